```python
import jax, jax.numpy as jnp
from jax import lax
import numpy as np

D_MODEL = 1024
BATCH = 8
SEQ = 8192
DEPTH = 1

D_RNN = 1024
RNN_BLOCKS = 8
RNN_BLOCK_W = D_RNN // RNN_BLOCKS
CONV_W = 4
LRU_C = 8.0
N_HEADS = 8
HEAD_DIM = 128
D_ATTN = N_HEADS * HEAD_DIM
ROT_DIM = HEAD_DIM // 4
ROPE_THETA = 500000.0
DILATED_GROUPS = ((128, 1), (512, 4), (2048, 16))
NEG_INF = -1e30
IN_WIDTHS = (D_RNN, D_RNN, D_ATTN, D_ATTN, D_ATTN, D_ATTN, D_MODEL, D_MODEL)
D_IN = sum(IN_WIDTHS)
IN_SPLITS = [int(s) for s in np.cumsum(IN_WIDTHS)[:-1]]
NORM_EPS = 1e-6

kernel_name = "hybrid_rglru_dilated_attn_block"


def rms_norm(x, g):
    xf = x.astype(jnp.float32)
    y = xf * lax.rsqrt(jnp.mean(xf * xf, axis=-1, keepdims=True) + NORM_EPS)
    return (y * g.astype(jnp.float32)).astype(x.dtype)


def causal_depthwise_conv(x, w, b):
    y = lax.conv_general_dilated(
        x, w[:, None, :], window_strides=(1,), padding=[(CONV_W - 1, 0)],
        dimension_numbers=("NWC", "WIO", "NWC"), feature_group_count=x.shape[-1])
    return y + b


def rg_lru(x, w_a, b_a, w_x, b_x, lam, positions):
    B, S, _ = x.shape
    xf = x.astype(jnp.float32)
    xh = xf.reshape(B, S, RNN_BLOCKS, RNN_BLOCK_W)
    r = jax.nn.sigmoid(jnp.einsum("bshi,hij->bshj", xh, w_a.astype(jnp.float32)) + b_a.astype(jnp.float32))
    i = jax.nn.sigmoid(jnp.einsum("bshi,hij->bshj", xh, w_x.astype(jnp.float32)) + b_x.astype(jnp.float32))
    r = r.reshape(B, S, D_RNN)
    i = i.reshape(B, S, D_RNN)
    log_a = -LRU_C * r * jax.nn.softplus(-lam.astype(jnp.float32))
    reset = (positions == 0)[..., None]
    a = jnp.where(reset, 0.0, jnp.exp(log_a))
    mult = jnp.where(reset, 1.0, jnp.sqrt(-jnp.expm1(2.0 * log_a)))
    bx = mult * i * xf

    def combine(left, right):
        a1, b1 = left
        a2, b2 = right
        return a1 * a2, a2 * b1 + b2

    _, h = lax.associative_scan(combine, (a, bx), axis=1)
    return h


def apply_partial_rope(t, cos, sin):
    tf = t.astype(jnp.float32)
    half = ROT_DIM // 2
    x1 = tf[..., :half]
    x2 = tf[..., half:ROT_DIM]
    return jnp.concatenate([x1 * cos - x2 * sin, x2 * cos + x1 * sin, tf[..., ROT_DIM:]], axis=-1)


def dilated_window_attention(q, k, v, window, dilation):
    B, S, H, Dh = q.shape
    blk = window // dilation
    span = blk * dilation
    s_pad = -(-S // span) * span
    nb = s_pad // span

    def to_blocks(t):
        t = jnp.pad(t, ((0, 0), (0, s_pad - S), (0, 0), (0, 0)))
        return t.reshape(B, nb, blk, dilation, H, Dh)

    qb, kb, vb = to_blocks(q), to_blocks(k), to_blocks(v)
    pad_prev = ((0, 0), (1, 0), (0, 0), (0, 0), (0, 0), (0, 0))
    kk = jnp.concatenate([jnp.pad(kb[:, :-1], pad_prev), kb], axis=2)
    vv = jnp.concatenate([jnp.pad(vb[:, :-1], pad_prev), vb], axis=2)
    s = jnp.einsum("bnqrhd,bnkrhd->bnrhqk", qb, kk) * (HEAD_DIM ** -0.5)
    qi = jnp.arange(blk)[:, None]
    ki = jnp.arange(2 * blk)[None, :]
    dist = blk + qi - ki
    band = (dist >= 0) & (dist <= blk)
    n_idx = jnp.arange(nb)[:, None, None]
    valid = band[None] & ((n_idx > 0) | (ki[None] >= blk))
    s = jnp.where(valid[None, :, None, None], s, NEG_INF)
    lse = jax.nn.logsumexp(s, axis=-1)
    p = jnp.exp(s - lse[..., None])
    o = jnp.einsum("bnrhqk,bnkrhd->bnqrhd", p, vv)
    o = o.reshape(B, s_pad, H, Dh)[:, :S]
    lse = lse.transpose(0, 1, 4, 2, 3).reshape(B, s_pad, H)[:, :S]
    return o, lse


def dilated_attention_mixture(q, k, v):
    outs, lses = [], []
    for window, dilation in DILATED_GROUPS:
        o, l = dilated_window_attention(q, k, v, window, dilation)
        outs.append(o)
        lses.append(l)
    w = jax.nn.softmax(jnp.stack(lses, axis=0), axis=0)
    return jnp.einsum("gbsh,gbshd->bshd", w, jnp.stack(outs, axis=0))


def setup_inputs(seed: int = 0) -> dict:
    key = jax.random.key(seed)
    ks = jax.random.split(key, 20)
    f32 = jnp.float32
    nrm = lambda k, shape, scale: jax.random.normal(k, shape, f32) * scale
    x = jax.random.normal(ks[0], (BATCH, SEQ, D_MODEL), f32)
    c = jax.random.normal(ks[1], (BATCH, D_MODEL), f32)
    positions = jnp.broadcast_to(jnp.arange(SEQ, dtype=jnp.int32)[None, :], (BATCH, SEQ))
    g_norm = 1.0 + nrm(ks[2], (DEPTH, D_MODEL), 0.02)
    w_mod = nrm(ks[3], (DEPTH, D_MODEL, 3 * D_MODEL), 0.5 * D_MODEL ** -0.5)
    b_mod = nrm(ks[4], (DEPTH, 3 * D_MODEL), 0.01)
    w_in = nrm(ks[5], (DEPTH, D_MODEL, D_IN), D_MODEL ** -0.5)
    b_gate = nrm(ks[6], (DEPTH, 2 * D_MODEL), 0.01)
    conv_w = nrm(ks[7], (DEPTH, CONV_W, D_RNN), CONV_W ** -0.5)
    conv_b = nrm(ks[8], (DEPTH, D_RNN), 0.01)
    w_a = nrm(ks[9], (DEPTH, RNN_BLOCKS, RNN_BLOCK_W, RNN_BLOCK_W), RNN_BLOCK_W ** -0.5)
    b_a = nrm(ks[10], (DEPTH, RNN_BLOCKS, RNN_BLOCK_W), 0.01)
    w_x = nrm(ks[11], (DEPTH, RNN_BLOCKS, RNN_BLOCK_W, RNN_BLOCK_W), RNN_BLOCK_W ** -0.5)
    b_x = nrm(ks[12], (DEPTH, RNN_BLOCKS, RNN_BLOCK_W), 0.01)
    a0 = jax.random.uniform(ks[13], (DEPTH, D_RNN), f32, 0.9, 0.999)
    sig = a0 ** (1.0 / LRU_C)
    lam = jnp.log(sig) - jnp.log1p(-sig)
    w_out_rnn = nrm(ks[14], (DEPTH, D_RNN, D_MODEL), D_RNN ** -0.5)
    w_out_attn = nrm(ks[15], (DEPTH, D_ATTN, D_MODEL), D_ATTN ** -0.5)
    w_o = nrm(ks[16], (DEPTH, D_MODEL, D_MODEL), D_MODEL ** -0.5)
    g_final = 1.0 + nrm(ks[17], (D_MODEL,), 0.02)
    return {"x": x, "c": c, "positions": positions, "g_norm": g_norm, "w_mod": w_mod,
            "b_mod": b_mod, "w_in": w_in, "b_gate": b_gate, "conv_w": conv_w, "conv_b": conv_b,
            "w_a": w_a, "b_a": b_a, "w_x": w_x, "b_x": b_x, "lam": lam, "w_out_rnn": w_out_rnn,
            "w_out_attn": w_out_attn, "w_o": w_o, "g_final": g_final}


def reference(x, c, positions, g_norm, w_mod, b_mod, w_in, b_gate, conv_w, conv_b,
              w_a, b_a, w_x, b_x, lam, w_out_rnn, w_out_attn, w_o, g_final):
    B, S, _ = x.shape
    dt = x.dtype
    inv_freq = ROPE_THETA ** (-jnp.arange(0, ROT_DIM, 2, dtype=jnp.float32) / ROT_DIM)
    ang = positions.astype(jnp.float32)[..., None] * inv_freq
    cos = jnp.cos(ang)[:, :, None, :]
    sin = jnp.sin(ang)[:, :, None, :]
    c_act = jax.nn.silu(c)
    for l in range(DEPTH):
        mod = c_act @ w_mod[l] + b_mod[l]
        shift, scale, gate = jnp.split(mod, 3, axis=-1)
        h = rms_norm(x, g_norm[l]) * (1.0 + scale[:, None, :]) + shift[:, None, :]
        proj = h @ w_in[l]
        x_rnn, z_rnn, q, k, v, z_attn, g_r, g_a = jnp.split(proj, IN_SPLITS, axis=-1)
        xc = causal_depthwise_conv(x_rnn, conv_w[l], conv_b[l])
        hr = rg_lru(xc, w_a[l], b_a[l], w_x[l], b_x[l], lam[l], positions)
        y_rnn = (hr * jax.nn.silu(z_rnn.astype(jnp.float32))).astype(dt) @ w_out_rnn[l]
        qh = apply_partial_rope(q.reshape(B, S, N_HEADS, HEAD_DIM), cos, sin)
        kh = apply_partial_rope(k.reshape(B, S, N_HEADS, HEAD_DIM), cos, sin)
        vh = v.reshape(B, S, N_HEADS, HEAD_DIM).astype(jnp.float32)
        o = dilated_attention_mixture(qh, kh, vh).reshape(B, S, D_ATTN)
        y_attn = (o * jax.nn.silu(z_attn.astype(jnp.float32))).astype(dt) @ w_out_attn[l]
        bg_r, bg_a = jnp.split(b_gate[l], 2, axis=-1)
        merged = jax.nn.sigmoid(g_r + bg_r) * y_rnn + jax.nn.sigmoid(g_a + bg_a) * y_attn
        x = x + gate[:, None, :] * (merged @ w_o[l])
    return rms_norm(x, g_final)
```

```python
import functools

import numpy as np
import jax
import jax.numpy as jnp
from jax import lax
from jax.experimental import pallas as pl
from jax.experimental.pallas import tpu as pltpu

F32 = jnp.float32
BF16 = jnp.bfloat16

D_MODEL = 1024
N_HEADS = 8
HEAD_DIM = 128
ROT_DIM = 32
ROPE_THETA = 500000.0
CONV_W = 4
LRU_C = 8.0
RNN_BLOCKS = 8
RNN_BLOCK_W = 128
NORM_EPS = 1e-6
NEG_INF = -1e30

SPAN = 2048
N_CLASS = 16
CLASS_ROWS = SPAN // N_CLASS
WINDOW_KEYS = 128
TILE = 512
TILE_CLASS_ROWS = TILE // N_CLASS
TILES_PER_SPAN = SPAN // TILE
LANE = 128
SUBLANE = 8
VMEM_LIMIT = 56 * 1024 * 1024


def _sigmoid(x):
    return 1.0 / (1.0 + jnp.exp(-x))


def _silu(x):
    return x * _sigmoid(x)


def _rows_as_lanes_to_rows(row):
    return jnp.broadcast_to(row, (LANE, LANE)).T


def _mod_kernel(c_ref, w_ref, b_ref, o_ref):
    ca = _silu(c_ref[...])
    o_ref[...] = jnp.dot(ca.astype(BF16), w_ref[...], preferred_element_type=F32) + b_ref[...]


def _modulation(c, w_mod, b_mod):
    B = c.shape[0]
    return pl.pallas_call(
        _mod_kernel,
        out_shape=jax.ShapeDtypeStruct((B, 3 * D_MODEL), F32),
        compiler_params=pltpu.CompilerParams(vmem_limit_bytes=VMEM_LIMIT),
        name="adaln_mod",
    )(c, w_mod.astype(BF16), b_mod.reshape(1, -1))


def _inproj_kernel(x_ref, shift_ref, scale_ref, gnorm_ref, pos_ref, rope_ref, wnat_ref, wperm_ref,
                   xr_ref, zr_ref, gr_ref, q_ref, k_ref, v_ref, za_ref, ga_ref,
                   slab_ref, pslab_ref, hp_ref):
    i = pl.program_id(1)
    x = x_ref[0]
    ms = jnp.mean(x * x, axis=-1, keepdims=True)
    h = (x * lax.rsqrt(ms + NORM_EPS)) * gnorm_ref[...]
    h = h * (1.0 + scale_ref[0]) + shift_ref[0]
    hb = h.astype(BF16)

    for c, ref in enumerate((xr_ref, zr_ref, gr_ref)):
        ref[0] = jnp.dot(hb, wnat_ref[:, c * D_MODEL:(c + 1) * D_MODEL],
                         preferred_element_type=F32).astype(BF16)

    for hh in range(N_HEADS):
        slab_ref[hh] = h[:, hh * HEAD_DIM:(hh + 1) * HEAD_DIM]
    for hh in range(N_HEADS):
        for r in range(N_CLASS):
            hp_ref[r * TILE_CLASS_ROWS:(r + 1) * TILE_CLASS_ROWS, hh * HEAD_DIM:(hh + 1) * HEAD_DIM] = (
                slab_ref[hh, pl.ds(r, TILE_CLASS_ROWS, stride=N_CLASS), :].astype(BF16))
    hp = hp_ref[...]

    for c in range(TILE // LANE):
        prow = pos_ref[0, pl.ds(i * (TILE // LANE) + c, 1), :].astype(F32)
        pslab_ref[c * LANE:(c + 1) * LANE, :] = _rows_as_lanes_to_rows(prow)
    pp = jnp.concatenate(
        [pslab_ref[pl.ds(r, TILE_CLASS_ROWS, stride=N_CLASS), :] for r in range(N_CLASS)], axis=0)
    ang = pp * rope_ref[0:1, :]
    cos_t = jnp.cos(ang)
    sin_t = jnp.sin(ang)
    sin_lo = sin_t * rope_ref[1:2, :]
    sin_hi = sin_t * rope_ref[2:3, :]

    def rope(t):
        return t * cos_t + pltpu.roll(t, HEAD_DIM - ROT_DIM // 2, 1) * sin_lo + pltpu.roll(t, ROT_DIM // 2, 1) * sin_hi

    def store_heads(ref, res, rotary, scale):
        for hh in range(N_HEADS):
            t = res[:, hh * HEAD_DIM:(hh + 1) * HEAD_DIM]
            if rotary:
                t = rope(t)
            if scale is not None:
                t = t * scale
            t = t.astype(BF16)
            for r in range(N_CLASS):
                ref[0, 0, hh, r] = t[r * TILE_CLASS_ROWS:(r + 1) * TILE_CLASS_ROWS]

    def store_rows(ref, res):
        t = res.astype(BF16)
        for r in range(N_CLASS):
            ref[0, 0, r] = t[r * TILE_CLASS_ROWS:(r + 1) * TILE_CLASS_ROWS]

    def proj(c):
        return jnp.dot(hp, wperm_ref[:, c * D_MODEL:(c + 1) * D_MODEL], preferred_element_type=F32)

    store_heads(q_ref, proj(0), True, HEAD_DIM ** -0.5)
    store_heads(k_ref, proj(1), True, None)
    store_heads(v_ref, proj(2), False, None)
    store_rows(za_ref, proj(3))
    store_rows(ga_ref, proj(4))


def _rope_lane_table():
    inv_freq = ROPE_THETA ** (-np.arange(0, ROT_DIM, 2, dtype=np.float64) / ROT_DIM)
    half = ROT_DIM // 2
    t = np.zeros((SUBLANE, HEAD_DIM), np.float32)
    t[0, :half] = inv_freq
    t[0, half:ROT_DIM] = inv_freq
    t[1, :half] = -1.0
    t[2, half:ROT_DIM] = 1.0
    return jnp.asarray(t)


def _in_projection(x, shift, scale, g_norm, pos3, w_nat, w_perm):
    B, S, D = x.shape
    n_span = S // SPAN
    n_tiles = S // TILE
    tok = jax.ShapeDtypeStruct((B, S, D), BF16)
    head = jax.ShapeDtypeStruct((B, n_span, N_HEADS, N_CLASS, CLASS_ROWS, HEAD_DIM), BF16)
    rows = jax.ShapeDtypeStruct((B, n_span, N_CLASS, CLASS_ROWS, D), BF16)
    tok_spec = pl.BlockSpec((1, TILE, D), lambda b, i: (b, i, 0))
    vec_spec = pl.BlockSpec((1, 1, D), lambda b, i: (b, 0, 0))
    head_spec = pl.BlockSpec((1, 1, N_HEADS, N_CLASS, TILE_CLASS_ROWS, HEAD_DIM),
                             lambda b, i: (b, i // TILES_PER_SPAN, 0, 0, i % TILES_PER_SPAN, 0))
    rows_spec = pl.BlockSpec((1, 1, N_CLASS, TILE_CLASS_ROWS, D),
                             lambda b, i: (b, i // TILES_PER_SPAN, 0, i % TILES_PER_SPAN, 0))
    const = lambda shape: pl.BlockSpec(shape, lambda b, i: (0,) * len(shape), pipeline_mode=pl.Buffered(1))
    return pl.pallas_call(
        _inproj_kernel,
        grid=(B, n_tiles),
        in_specs=[tok_spec, vec_spec, vec_spec, const((1, D)),
                  pl.BlockSpec((1, S // LANE, LANE), lambda b, i: (b, 0, 0)),
                  const((SUBLANE, HEAD_DIM)), const((D, 3 * D)), const((D, 5 * D))],
        out_specs=[tok_spec, tok_spec, tok_spec, head_spec, head_spec, head_spec, rows_spec, rows_spec],
        out_shape=[tok, tok, tok, head, head, head, rows, rows],
        scratch_shapes=[pltpu.VMEM((N_HEADS, TILE, HEAD_DIM), F32),
                        pltpu.VMEM((TILE, LANE), F32),
                        pltpu.VMEM((TILE, D), BF16)],
        compiler_params=pltpu.CompilerParams(dimension_semantics=("arbitrary", "arbitrary"),
                                             vmem_limit_bytes=VMEM_LIMIT),
        name="in_projection",
    )(x, shift, scale, g_norm, pos3, _rope_lane_table(), w_nat, w_perm)


def _rnn_kernel(xr_ref, zr_ref, gr_ref, pos_ref, convw_ref, convb_ref, wg_ref, ba_ref, bx_ref, lam_ref,
                bgr_ref, wout_ref, out_ref, xbuf_ref, a_ref, b_ref, carry_ref):
    i = pl.program_id(1)

    @pl.when(i == 0)
    def _():
        xbuf_ref[0:SUBLANE, :] = jnp.zeros((SUBLANE, D_MODEL), F32)
        carry_ref[...] = jnp.zeros((1, D_MODEL), F32)

    xbuf_ref[SUBLANE:SUBLANE + TILE, :] = xr_ref[0].astype(F32)
    xc = convb_ref[...]
    for k in range(CONV_W):
        off = SUBLANE - (CONV_W - 1) + k
        xc = xc + convw_ref[k:k + 1, :] * xbuf_ref[off:off + TILE, :]
    xbuf_ref[0:SUBLANE, :] = xbuf_ref[TILE:TILE + SUBLANE, :]

    xcb = xc.astype(BF16)
    rs, gs = [], []
    for blk in range(RNN_BLOCKS):
        g = jnp.dot(xcb[:, blk * RNN_BLOCK_W:(blk + 1) * RNN_BLOCK_W], wg_ref[blk],
                    preferred_element_type=F32)
        rs.append(g[:, :RNN_BLOCK_W])
        gs.append(g[:, RNN_BLOCK_W:])
    r = _sigmoid(jnp.concatenate(rs, axis=1) + ba_ref[...])
    ig = _sigmoid(jnp.concatenate(gs, axis=1) + bx_ref[...])

    nlam = -lam_ref[...]
    softplus = jnp.maximum(nlam, 0.0) + jnp.log(1.0 + jnp.exp(-jnp.abs(nlam)))
    a = jnp.exp((-LRU_C) * r * softplus)
    mult = jnp.sqrt(1.0 - a * a)

    resets = []
    for c in range(TILE // LANE):
        prow = pos_ref[0, pl.ds(i * (TILE // LANE) + c, 1), :]
        resets.append(_rows_as_lanes_to_rows((prow == 0).astype(F32)))
    reset = jnp.concatenate(resets, axis=0) > 0.5
    reset = jnp.concatenate([reset] * (D_MODEL // LANE), axis=1)
    a = jnp.where(reset, 0.0, a)
    mult = jnp.where(reset, 1.0, mult)
    bv = mult * ig * xc

    row = lax.broadcasted_iota(jnp.int32, (TILE, D_MODEL), 0) & (SUBLANE - 1)
    for d in (1, 2, 4):
        a_prev = pltpu.roll(a, d, 0)
        b_prev = pltpu.roll(bv, d, 0)
        take = row >= d
        bv = jnp.where(take, a * b_prev + bv, bv)
        a = jnp.where(take, a * a_prev, a)
    a_ref[...] = a
    b_ref[...] = bv

    def chain(g, carry):
        sl = pl.ds(pl.multiple_of(g * SUBLANE, SUBLANE), SUBLANE)
        hg = a_ref[sl, :] * carry + b_ref[sl, :]
        b_ref[sl, :] = hg
        return hg[SUBLANE - 1:SUBLANE, :]

    carry_ref[...] = lax.fori_loop(0, TILE // SUBLANE, chain, carry_ref[...])
    hr = b_ref[...]

    y = (hr * _silu(zr_ref[0].astype(F32))).astype(BF16)
    y = jnp.dot(y, wout_ref[...], preferred_element_type=F32)
    gate = _sigmoid(gr_ref[0].astype(F32) + bgr_ref[...])
    out_ref[0] = (gate * y).astype(BF16)


def _rnn_branch(xr, zr, gr, pos3, conv_w, conv_b, w_gates, b_a, b_x, lam, bg_r, w_out):
    B, S, D = xr.shape
    tok_spec = pl.BlockSpec((1, TILE, D), lambda b, i: (b, i, 0))
    const = lambda shape: pl.BlockSpec(shape, lambda b, i: (0,) * len(shape))
    return pl.pallas_call(
        _rnn_kernel,
        grid=(B, S // TILE),
        in_specs=[tok_spec, tok_spec, tok_spec,
                  pl.BlockSpec((1, S // LANE, LANE), lambda b, i: (b, 0, 0)),
                  const((CONV_W, D)), const((1, D)), const((RNN_BLOCKS, RNN_BLOCK_W, 2 * RNN_BLOCK_W)),
                  const((1, D)), const((1, D)), const((1, D)), const((1, D)), const((D, D))],
        out_specs=tok_spec,
        out_shape=jax.ShapeDtypeStruct((B, S, D), BF16),
        scratch_shapes=[pltpu.VMEM((TILE + SUBLANE, D), F32),
                        pltpu.VMEM((TILE, D), F32),
                        pltpu.VMEM((TILE, D), F32),
                        pltpu.VMEM((1, D), F32)],
        compiler_params=pltpu.CompilerParams(dimension_semantics=("arbitrary", "arbitrary"),
                                             vmem_limit_bytes=VMEM_LIMIT),
        name="rglru_branch",
    )(xr, zr, gr, pos3, conv_w, conv_b, w_gates, b_a, b_x, lam, bg_r, w_out)


G4_QROWS = 32
G1_QROWS = 16


def _band_bias_tables():
    def table(n_cls, q_rows):
        k_rows = 2 * q_rows
        cq, jq = np.meshgrid(np.arange(n_cls), np.arange(q_rows), indexing="ij")
        ck, jk = np.meshgrid(np.arange(n_cls), np.arange(k_rows), indexing="ij")
        cq, jq, ck, jk = cq.reshape(-1, 1), jq.reshape(-1, 1), ck.reshape(1, -1), jk.reshape(1, -1)
        dist = n_cls * (jq + q_rows - jk) + (cq - ck)
        valid = (dist >= 0) & (dist <= WINDOW_KEYS)
        first = valid & (jk >= q_rows)
        return np.stack([np.where(valid, 0.0, NEG_INF), np.where(first, 0.0, NEG_INF)]).astype(np.float32)

    b16 = table(1, CLASS_ROWS)
    b4 = table(4, G4_QROWS)
    b1 = table(16, G1_QROWS)
    return jnp.asarray(b16), jnp.asarray(b4), jnp.asarray(b1)


def _attend(qb, kb, vb, bias):
    s = lax.dot_general(qb, kb, (((1,), (1,)), ((), ())), preferred_element_type=F32) + bias
    m = jnp.max(s, axis=-1, keepdims=True)
    p = jnp.exp(s - m)
    l = jnp.sum(p, axis=-1, keepdims=True)
    o = jnp.dot(p.astype(BF16), vb, preferred_element_type=F32) / l
    return o, m + jnp.log(l)


def _attn_kernel(q_ref, kc_ref, kp_ref, vc_ref, vp_ref, b16_ref, b4_ref, b1_ref, o_ref,
                 kk_ref, vv_ref, o16_ref, o4_ref, o1_ref, l16_ref, l4_ref, l1_ref):
    first_span = (pl.program_id(1) == 0).astype(jnp.int32)

    for r in range(N_CLASS):
        kk_ref[r, 0:CLASS_ROWS] = kp_ref[0, 0, 0, r]
        kk_ref[r, CLASS_ROWS:2 * CLASS_ROWS] = kc_ref[0, 0, 0, r]
        vv_ref[r, 0:CLASS_ROWS] = vp_ref[0, 0, 0, r]
        vv_ref[r, CLASS_ROWS:2 * CLASS_ROWS] = vc_ref[0, 0, 0, r]

    bias16 = b16_ref[first_span]

    def group16(r, _):
        o, lse = _attend(q_ref[0, 0, 0, r], kk_ref[r], vv_ref[r], bias16)
        o16_ref[r] = o
        l16_ref[r] = lse
        return 0

    lax.fori_loop(0, N_CLASS, group16, 0)

    def group4(it, _):
        r4 = it // (CLASS_ROWS // G4_QROWS)
        qi = it % (CLASS_ROWS // G4_QROWS)
        q0 = pl.multiple_of(qi * G4_QROWS, G4_QROWS)
        k0 = pl.multiple_of(CLASS_ROWS - G4_QROWS + qi * G4_QROWS, G4_QROWS)
        qb = jnp.concatenate([q_ref[0, 0, 0, r4 + 4 * c, pl.ds(q0, G4_QROWS), :] for c in range(4)], axis=0)
        kb = jnp.concatenate([kk_ref[r4 + 4 * c, pl.ds(k0, 2 * G4_QROWS), :] for c in range(4)], axis=0)
        vb = jnp.concatenate([vv_ref[r4 + 4 * c, pl.ds(k0, 2 * G4_QROWS), :] for c in range(4)], axis=0)
        bias = b4_ref[first_span * (qi == 0).astype(jnp.int32)]
        o, lse = _attend(qb, kb, vb, bias)
        for c in range(4):
            o4_ref[r4 + 4 * c, pl.ds(q0, G4_QROWS), :] = o[c * G4_QROWS:(c + 1) * G4_QROWS]
            l4_ref[r4 + 4 * c, pl.ds(q0, G4_QROWS), :] = lse[c * G4_QROWS:(c + 1) * G4_QROWS]
        return 0

    lax.fori_loop(0, 4 * (CLASS_ROWS // G4_QROWS), group4, 0)

    def group1(u, _):
        q0 = pl.multiple_of(u * G1_QROWS, G1_QROWS)
        k0 = pl.multiple_of(CLASS_ROWS - G1_QROWS + u * G1_QROWS, G1_QROWS)
        qb = jnp.concatenate([q_ref[0, 0, 0, r, pl.ds(q0, G1_QROWS), :] for r in range(N_CLASS)], axis=0)
        kb = jnp.concatenate([kk_ref[r, pl.ds(k0, 2 * G1_QROWS), :] for r in range(N_CLASS)], axis=0)
        vb = jnp.concatenate([vv_ref[r, pl.ds(k0, 2 * G1_QROWS), :] for r in range(N_CLASS)], axis=0)
        bias = b1_ref[first_span * (u == 0).astype(jnp.int32)]
        o, lse = _attend(qb, kb, vb, bias)
        for r in range(N_CLASS):
            o1_ref[r, pl.ds(q0, G1_QROWS), :] = o[r * G1_QROWS:(r + 1) * G1_QROWS]
            l1_ref[r, pl.ds(q0, G1_QROWS), :] = lse[r * G1_QROWS:(r + 1) * G1_QROWS]
        return 0

    lax.fori_loop(0, CLASS_ROWS // G1_QROWS, group1, 0)

    def merge(r, _):
        la, lb, lc = l16_ref[r], l4_ref[r], l1_ref[r]
        m = jnp.maximum(jnp.maximum(la, lb), lc)
        wa, wb, wc = jnp.exp(la - m), jnp.exp(lb - m), jnp.exp(lc - m)
        o = (wa * o16_ref[r] + wb * o4_ref[r] + wc * o1_ref[r]) / (wa + wb + wc)
        o_ref[0, 0, 0, r] = o.astype(BF16)
        return 0

    lax.fori_loop(0, N_CLASS, merge, 0)


def _attention(q, k, v):
    B, n_span = q.shape[0], q.shape[1]
    blk = (1, 1, 1, N_CLASS, CLASS_ROWS, HEAD_DIM)
    cur = pl.BlockSpec(blk, lambda b, n, h: (b, n, h, 0, 0, 0))
    prev = pl.BlockSpec(blk, lambda b, n, h: (b, jnp.maximum(n - 1, 0), h, 0, 0, 0))
    b16, b4, b1 = _band_bias_tables()
    const = lambda a: pl.BlockSpec(a.shape, lambda b, n, h: (0,) * a.ndim)
    acc = pltpu.VMEM((N_CLASS, CLASS_ROWS, HEAD_DIM), F32)
    stat = pltpu.VMEM((N_CLASS, CLASS_ROWS, 1), F32)
    kv = pltpu.VMEM((N_CLASS, 2 * CLASS_ROWS, HEAD_DIM), BF16)
    return pl.pallas_call(
        _attn_kernel,
        grid=(B, n_span, N_HEADS),
        in_specs=[cur, cur, prev, cur, prev, const(b16), const(b4), const(b1)],
        out_specs=cur,
        out_shape=jax.ShapeDtypeStruct(q.shape, BF16),
        scratch_shapes=[kv, kv, acc, acc, acc, stat, stat, stat],
        compiler_params=pltpu.CompilerParams(dimension_semantics=("arbitrary", "arbitrary", "arbitrary"),
                                             vmem_limit_bytes=VMEM_LIMIT),
        name="dilated_attention",
    )(q, k, k, v, v, b16, b4, b1)


def _out_kernel(o_ref, za_ref, ga_ref, yr_ref, x_ref, gate_ref, bga_ref, wa_ref, wo_ref, gfin_ref,
                out_ref, op_ref, slab_ref, *, final_norm):
    for hh in range(N_HEADS):
        for r in range(N_CLASS):
            op_ref[r * TILE_CLASS_ROWS:(r + 1) * TILE_CLASS_ROWS, hh * HEAD_DIM:(hh + 1) * HEAD_DIM] = (
                o_ref[0, 0, hh, r])
    za = jnp.concatenate([za_ref[0, 0, r] for r in range(N_CLASS)], axis=0).astype(F32)
    ga = jnp.concatenate([ga_ref[0, 0, r] for r in range(N_CLASS)], axis=0).astype(F32)
    t = (op_ref[...].astype(F32) * _silu(za)).astype(BF16)
    ya = jnp.dot(t, wa_ref[...], preferred_element_type=F32)
    ya = _sigmoid(ga + bga_ref[...]) * ya

    for hh in range(N_HEADS):
        for r in range(N_CLASS):
            slab_ref[hh, pl.ds(r, TILE_CLASS_ROWS, stride=N_CLASS), :] = (
                ya[r * TILE_CLASS_ROWS:(r + 1) * TILE_CLASS_ROWS, hh * HEAD_DIM:(hh + 1) * HEAD_DIM])
    ya_tok = jnp.concatenate([slab_ref[hh] for hh in range(N_HEADS)], axis=1)

    merged = (ya_tok + yr_ref[0].astype(F32)).astype(BF16)
    y = x_ref[0] + gate_ref[0] * jnp.dot(merged, wo_ref[...], preferred_element_type=F32)
    if final_norm:
        ms = jnp.mean(y * y, axis=-1, keepdims=True)
        y = (y * lax.rsqrt(ms + NORM_EPS)) * gfin_ref[...]
    out_ref[0] = y


def _output_stage(o, za, ga, yr, x, gate, bg_a, w_out_attn, w_o, g_final, final_norm):
    B, S, D = x.shape
    tok_spec = pl.BlockSpec((1, TILE, D), lambda b, i: (b, i, 0))
    head_spec = pl.BlockSpec((1, 1, N_HEADS, N_CLASS, TILE_CLASS_ROWS, HEAD_DIM),
                             lambda b, i: (b, i // TILES_PER_SPAN, 0, 0, i % TILES_PER_SPAN, 0))
    rows_spec = pl.BlockSpec((1, 1, N_CLASS, TILE_CLASS_ROWS, D),
                             lambda b, i: (b, i // TILES_PER_SPAN, 0, i % TILES_PER_SPAN, 0))
    const = lambda shape: pl.BlockSpec(shape, lambda b, i: (0,) * len(shape))
    return pl.pallas_call(
        functools.partial(_out_kernel, final_norm=final_norm),
        grid=(B, S // TILE),
        in_specs=[head_spec, rows_spec, rows_spec, tok_spec, tok_spec,
                  pl.BlockSpec((1, 1, D), lambda b, i: (b, 0, 0)),
                  const((1, D)), const((D, D)), const((D, D)), const((1, D))],
        out_specs=tok_spec,
        out_shape=jax.ShapeDtypeStruct((B, S, D), x.dtype),
        scratch_shapes=[pltpu.VMEM((TILE, D), BF16),
                        pltpu.VMEM((N_HEADS, TILE, HEAD_DIM), F32)],
        compiler_params=pltpu.CompilerParams(dimension_semantics=("arbitrary", "arbitrary"),
                                             vmem_limit_bytes=VMEM_LIMIT),
        name="output_stage",
    )(o, za, ga, yr, x, gate, bg_a, w_out_attn, w_o, g_final)


def kernel(x, c, positions, g_norm, w_mod, b_mod, w_in, b_gate, conv_w, conv_b, w_a, b_a, w_x, b_x, lam,
           w_out_rnn, w_out_attn, w_o, g_final):
    B, S, D = x.shape
    depth = g_norm.shape[0]
    assert D == D_MODEL and S % SPAN == 0
    pos3 = positions.reshape(B, S // LANE, LANE)
    for l in range(depth):
        mod = _modulation(c, w_mod[l], b_mod[l])
        shift, scale, gate = (m.reshape(B, 1, D) for m in jnp.split(mod, 3, axis=-1))

        wl = w_in[l].astype(BF16)
        cols = [wl[:, j * D:(j + 1) * D] for j in range(8)]
        w_nat = jnp.concatenate([cols[0], cols[1], cols[6]], axis=1)
        w_perm = jnp.concatenate([cols[2], cols[3], cols[4], cols[5], cols[7]], axis=1)
        xr, zr, gr, q, k, v, za, ga = _in_projection(
            x, shift, scale, g_norm[l].reshape(1, D), pos3, w_nat, w_perm)

        w_gates = jnp.concatenate([w_a[l], w_x[l]], axis=-1).astype(BF16)
        yr = _rnn_branch(xr, zr, gr, pos3, conv_w[l], conv_b[l].reshape(1, D), w_gates,
                         b_a[l].reshape(1, D), b_x[l].reshape(1, D), lam[l].reshape(1, D),
                         b_gate[l, :D].reshape(1, D), w_out_rnn[l].astype(BF16))

        o = _attention(q, k, v)

        x = _output_stage(o, za, ga, yr, x, gate, b_gate[l, D:].reshape(1, D),
                          w_out_attn[l].astype(BF16), w_o[l].astype(BF16), g_final.reshape(1, D),
                          final_norm=(l == depth - 1))
    return x
```

```python
import functools

import numpy as np
import jax
import jax.numpy as jnp
from jax import lax
from jax.experimental import pallas as pl
from jax.experimental.pallas import tpu as pltpu

F32 = jnp.float32
BF16 = jnp.bfloat16

D_MODEL = 1024
N_HEADS = 8
HEAD_DIM = 128
ROT_DIM = 32
ROPE_THETA = 500000.0
CONV_W = 4
LRU_C = 8.0
RNN_BLOCKS = 8
RNN_BLOCK_W = 128
NORM_EPS = 1e-6
NEG_INF = -1e30

SPAN = 2048
N_CLASS = 16
CLASS_ROWS = SPAN // N_CLASS
WINDOW_KEYS = 128
TILE = 512
TILE_CLASS_ROWS = TILE // N_CLASS
TILES_PER_SPAN = SPAN // TILE
LANE = 128
SUBLANE = 8
VMEM_LIMIT = 56 * 1024 * 1024


def _sigmoid(x):
    return 1.0 / (1.0 + jnp.exp(-x))


def _silu(x):
    return x * _sigmoid(x)


def _rows_as_lanes_to_rows(row):
    return jnp.broadcast_to(row, (LANE, LANE)).T


def _mod_kernel(c_ref, w_ref, b_ref, o_ref):
    ca = _silu(c_ref[...])
    o_ref[...] = jnp.dot(ca.astype(BF16), w_ref[...], preferred_element_type=F32) + b_ref[...]


def _modulation(c, w_mod, b_mod):
    B = c.shape[0]
    return pl.pallas_call(
        _mod_kernel,
        out_shape=jax.ShapeDtypeStruct((B, 3 * D_MODEL), F32),
        compiler_params=pltpu.CompilerParams(vmem_limit_bytes=VMEM_LIMIT),
        name="adaln_mod",
    )(c, w_mod.astype(BF16), b_mod.reshape(1, -1))


def _inproj_kernel(x_ref, shift_ref, scale_ref, gnorm_ref, pos_ref, rope_ref, wnat_ref, wperm_ref,
                   xr_ref, zr_ref, gr_ref, q_ref, k_ref, v_ref, za_ref, ga_ref,
                   slab_ref, pslab_ref, hp_ref):
    i = pl.program_id(1)
    x = x_ref[0]
    ms = jnp.mean(x * x, axis=-1, keepdims=True)
    h = (x * lax.rsqrt(ms + NORM_EPS)) * gnorm_ref[...]
    h = h * (1.0 + scale_ref[0]) + shift_ref[0]
    hb = h.astype(BF16)

    for c, ref in enumerate((xr_ref, zr_ref, gr_ref)):
        ref[0] = jnp.dot(hb, wnat_ref[:, c * D_MODEL:(c + 1) * D_MODEL],
                         preferred_element_type=F32).astype(BF16)

    for hh in range(N_HEADS):
        slab_ref[hh] = h[:, hh * HEAD_DIM:(hh + 1) * HEAD_DIM]
    for hh in range(N_HEADS):
        for r in range(N_CLASS):
            hp_ref[r * TILE_CLASS_ROWS:(r + 1) * TILE_CLASS_ROWS, hh * HEAD_DIM:(hh + 1) * HEAD_DIM] = (
                slab_ref[hh, pl.ds(r, TILE_CLASS_ROWS, stride=N_CLASS), :].astype(BF16))
    hp = hp_ref[...]

    for c in range(TILE // LANE):
        prow = pos_ref[0, pl.ds(i * (TILE // LANE) + c, 1), :].astype(F32)
        pslab_ref[c * LANE:(c + 1) * LANE, :] = _rows_as_lanes_to_rows(prow)
    pp = jnp.concatenate(
        [pslab_ref[pl.ds(r, TILE_CLASS_ROWS, stride=N_CLASS), :] for r in range(N_CLASS)], axis=0)
    ang = pp * rope_ref[0:1, :]
    cos_t = jnp.cos(ang)
    sin_t = jnp.sin(ang)
    sin_lo = sin_t * rope_ref[1:2, :]
    sin_hi = sin_t * rope_ref[2:3, :]

    def rope(t):
        return t * cos_t + pltpu.roll(t, HEAD_DIM - ROT_DIM // 2, 1) * sin_lo + pltpu.roll(t, ROT_DIM // 2, 1) * sin_hi

    def store_heads(ref, res, rotary, scale):
        for hh in range(N_HEADS):
            t = res[:, hh * HEAD_DIM:(hh + 1) * HEAD_DIM]
            if rotary:
                t = rope(t)
            if scale is not None:
                t = t * scale
            t = t.astype(BF16)
            for r in range(N_CLASS):
                ref[0, 0, hh, r] = t[r * TILE_CLASS_ROWS:(r + 1) * TILE_CLASS_ROWS]

    def store_rows(ref, res):
        t = res.astype(BF16)
        for r in range(N_CLASS):
            ref[0, 0, r] = t[r * TILE_CLASS_ROWS:(r + 1) * TILE_CLASS_ROWS]

    def proj(c):
        return jnp.dot(hp, wperm_ref[:, c * D_MODEL:(c + 1) * D_MODEL], preferred_element_type=F32)

    store_heads(q_ref, proj(0), True, HEAD_DIM ** -0.5)
    store_heads(k_ref, proj(1), True, None)
    store_heads(v_ref, proj(2), False, None)
    store_rows(za_ref, proj(3))
    store_rows(ga_ref, proj(4))


def _rope_lane_table():
    inv_freq = ROPE_THETA ** (-np.arange(0, ROT_DIM, 2, dtype=np.float64) / ROT_DIM)
    half = ROT_DIM // 2
    t = np.zeros((SUBLANE, HEAD_DIM), np.float32)
    t[0, :half] = inv_freq
    t[0, half:ROT_DIM] = inv_freq
    t[1, :half] = -1.0
    t[2, half:ROT_DIM] = 1.0
    return jnp.asarray(t)


def _in_projection(x, shift, scale, g_norm, pos3, w_nat, w_perm):
    B, S, D = x.shape
    n_span = S // SPAN
    n_tiles = S // TILE
    tok = jax.ShapeDtypeStruct((B, S, D), BF16)
    head = jax.ShapeDtypeStruct((B, n_span, N_HEADS, N_CLASS, CLASS_ROWS, HEAD_DIM), BF16)
    rows = jax.ShapeDtypeStruct((B, n_span, N_CLASS, CLASS_ROWS, D), BF16)
    tok_spec = pl.BlockSpec((1, TILE, D), lambda b, i: (b, i, 0))
    vec_spec = pl.BlockSpec((1, 1, D), lambda b, i: (b, 0, 0))
    head_spec = pl.BlockSpec((1, 1, N_HEADS, N_CLASS, TILE_CLASS_ROWS, HEAD_DIM),
                             lambda b, i: (b, i // TILES_PER_SPAN, 0, 0, i % TILES_PER_SPAN, 0))
    rows_spec = pl.BlockSpec((1, 1, N_CLASS, TILE_CLASS_ROWS, D),
                             lambda b, i: (b, i // TILES_PER_SPAN, 0, i % TILES_PER_SPAN, 0))
    const = lambda shape: pl.BlockSpec(shape, lambda b, i: (0,) * len(shape), pipeline_mode=pl.Buffered(1))
    return pl.pallas_call(
        _inproj_kernel,
        grid=(B, n_tiles),
        in_specs=[tok_spec, vec_spec, vec_spec, const((1, D)),
                  pl.BlockSpec((1, S // LANE, LANE), lambda b, i: (b, 0, 0)),
                  const((SUBLANE, HEAD_DIM)), const((D, 3 * D)), const((D, 5 * D))],
        out_specs=[tok_spec, tok_spec, tok_spec, head_spec, head_spec, head_spec, rows_spec, rows_spec],
        out_shape=[tok, tok, tok, head, head, head, rows, rows],
        scratch_shapes=[pltpu.VMEM((N_HEADS, TILE, HEAD_DIM), F32),
                        pltpu.VMEM((TILE, LANE), F32),
                        pltpu.VMEM((TILE, D), BF16)],
        compiler_params=pltpu.CompilerParams(dimension_semantics=("arbitrary", "arbitrary"),
                                             vmem_limit_bytes=VMEM_LIMIT),
        name="in_projection",
    )(x, shift, scale, g_norm, pos3, _rope_lane_table(), w_nat, w_perm)


def _rnn_kernel(xr_ref, zr_ref, gr_ref, pos_ref, convw_ref, convb_ref, wg_ref, ba_ref, bx_ref, lam_ref,
                bgr_ref, wout_ref, out_ref, xbuf_ref, a_ref, b_ref, carry_ref):
    i = pl.program_id(1)

    @pl.when(i == 0)
    def _():
        xbuf_ref[0:SUBLANE, :] = jnp.zeros((SUBLANE, D_MODEL), F32)
        carry_ref[...] = jnp.zeros((1, D_MODEL), F32)

    xbuf_ref[SUBLANE:SUBLANE + TILE, :] = xr_ref[0].astype(F32)
    xc = convb_ref[...]
    for k in range(CONV_W):
        off = SUBLANE - (CONV_W - 1) + k
        xc = xc + convw_ref[k:k + 1, :] * xbuf_ref[off:off + TILE, :]
    xbuf_ref[0:SUBLANE, :] = xbuf_ref[TILE:TILE + SUBLANE, :]

    xcb = xc.astype(BF16)
    rs, gs = [], []
    for blk in range(RNN_BLOCKS):
        g = jnp.dot(xcb[:, blk * RNN_BLOCK_W:(blk + 1) * RNN_BLOCK_W], wg_ref[blk],
                    preferred_element_type=F32)
        rs.append(g[:, :RNN_BLOCK_W])
        gs.append(g[:, RNN_BLOCK_W:])
    r = _sigmoid(jnp.concatenate(rs, axis=1) + ba_ref[...])
    ig = _sigmoid(jnp.concatenate(gs, axis=1) + bx_ref[...])

    nlam = -lam_ref[...]
    softplus = jnp.maximum(nlam, 0.0) + jnp.log(1.0 + jnp.exp(-jnp.abs(nlam)))
    a = jnp.exp((-LRU_C) * r * softplus)
    mult = jnp.sqrt(1.0 - a * a)

    resets = []
    for c in range(TILE // LANE):
        prow = pos_ref[0, pl.ds(i * (TILE // LANE) + c, 1), :]
        resets.append(_rows_as_lanes_to_rows((prow == 0).astype(F32)))
    reset = jnp.concatenate(resets, axis=0) > 0.5
    reset = jnp.concatenate([reset] * (D_MODEL // LANE), axis=1)
    a = jnp.where(reset, 0.0, a)
    mult = jnp.where(reset, 1.0, mult)
    bv = mult * ig * xc

    row = lax.broadcasted_iota(jnp.int32, (TILE, D_MODEL), 0) & (SUBLANE - 1)
    for d in (1, 2, 4):
        a_prev = pltpu.roll(a, d, 0)
        b_prev = pltpu.roll(bv, d, 0)
        take = row >= d
        bv = jnp.where(take, a * b_prev + bv, bv)
        a = jnp.where(take, a * a_prev, a)
    a_ref[...] = a
    b_ref[...] = bv

    def chain(g, carry):
        sl = pl.ds(pl.multiple_of(g * SUBLANE, SUBLANE), SUBLANE)
        hg = a_ref[sl, :] * carry + b_ref[sl, :]
        b_ref[sl, :] = hg
        return hg[SUBLANE - 1:SUBLANE, :]

    carry_ref[...] = lax.fori_loop(0, TILE // SUBLANE, chain, carry_ref[...])
    hr = b_ref[...]

    y = (hr * _silu(zr_ref[0].astype(F32))).astype(BF16)
    y = jnp.dot(y, wout_ref[...], preferred_element_type=F32)
    gate = _sigmoid(gr_ref[0].astype(F32) + bgr_ref[...])
    out_ref[0] = (gate * y).astype(BF16)


def _rnn_branch(xr, zr, gr, pos3, conv_w, conv_b, w_gates, b_a, b_x, lam, bg_r, w_out):
    B, S, D = xr.shape
    tok_spec = pl.BlockSpec((1, TILE, D), lambda b, i: (b, i, 0))
    const = lambda shape: pl.BlockSpec(shape, lambda b, i: (0,) * len(shape))
    return pl.pallas_call(
        _rnn_kernel,
        grid=(B, S // TILE),
        in_specs=[tok_spec, tok_spec, tok_spec,
                  pl.BlockSpec((1, S // LANE, LANE), lambda b, i: (b, 0, 0)),
                  const((CONV_W, D)), const((1, D)), const((RNN_BLOCKS, RNN_BLOCK_W, 2 * RNN_BLOCK_W)),
                  const((1, D)), const((1, D)), const((1, D)), const((1, D)), const((D, D))],
        out_specs=tok_spec,
        out_shape=jax.ShapeDtypeStruct((B, S, D), BF16),
        scratch_shapes=[pltpu.VMEM((TILE + SUBLANE, D), F32),
                        pltpu.VMEM((TILE, D), F32),
                        pltpu.VMEM((TILE, D), F32),
                        pltpu.VMEM((1, D), F32)],
        compiler_params=pltpu.CompilerParams(dimension_semantics=("arbitrary", "arbitrary"),
                                             vmem_limit_bytes=VMEM_LIMIT),
        name="rglru_branch",
    )(xr, zr, gr, pos3, conv_w, conv_b, w_gates, b_a, b_x, lam, bg_r, w_out)


G4_QROWS = 32
G1_QROWS = 16


def _band_bias_tables():
    def table(n_cls, q_rows):
        k_rows = 2 * q_rows
        cq, jq = np.meshgrid(np.arange(n_cls), np.arange(q_rows), indexing="ij")
        ck, jk = np.meshgrid(np.arange(n_cls), np.arange(k_rows), indexing="ij")
        cq, jq, ck, jk = cq.reshape(-1, 1), jq.reshape(-1, 1), ck.reshape(1, -1), jk.reshape(1, -1)
        dist = n_cls * (jq + q_rows - jk) + (cq - ck)
        valid = (dist >= 0) & (dist <= WINDOW_KEYS)
        first = valid & (jk >= q_rows)
        return np.stack([np.where(valid, 0.0, NEG_INF), np.where(first, 0.0, NEG_INF)]).astype(np.float32)

    b16 = table(1, CLASS_ROWS)
    b4 = table(4, G4_QROWS)
    b1 = table(16, G1_QROWS)
    return jnp.asarray(b16), jnp.asarray(b4), jnp.asarray(b1)


def _attend(qb, kb, vb, bias):
    s = lax.dot_general(qb, kb, (((1,), (1,)), ((), ())), preferred_element_type=F32) + bias
    m = jnp.max(s, axis=-1, keepdims=True)
    p = jnp.exp(s - m)
    l = jnp.sum(p, axis=-1, keepdims=True)
    o = jnp.dot(p.astype(BF16), vb, preferred_element_type=F32) / l
    return o, m + jnp.log(l)


def _attn_kernel(q_ref, kc_ref, kp_ref, vc_ref, vp_ref, b16_ref, b4_ref, b1_ref, o_ref,
                 kk_ref, vv_ref, o16_ref, o4_ref, o1_ref, l16_ref, l4_ref, l1_ref):
    first_span = (pl.program_id(1) == 0).astype(jnp.int32)

    for r in range(N_CLASS):
        kk_ref[r, 0:CLASS_ROWS] = kp_ref[0, 0, 0, r]
        kk_ref[r, CLASS_ROWS:2 * CLASS_ROWS] = kc_ref[0, 0, 0, r]
        vv_ref[r, 0:CLASS_ROWS] = vp_ref[0, 0, 0, r]
        vv_ref[r, CLASS_ROWS:2 * CLASS_ROWS] = vc_ref[0, 0, 0, r]

    bias16 = b16_ref[first_span]
    for r in range(N_CLASS):
        o, lse = _attend(q_ref[0, 0, 0, r], kk_ref[r], vv_ref[r], bias16)
        o16_ref[r] = o
        l16_ref[r] = lse

    bias4 = (b4_ref[first_span], b4_ref[0])
    for r4 in range(4):
        cls = [r4 + 4 * c for c in range(4)]
        for qi in range(CLASS_ROWS // G4_QROWS):
            q0 = qi * G4_QROWS
            k0 = CLASS_ROWS - G4_QROWS + q0
            qb = jnp.concatenate([q_ref[0, 0, 0, r, q0:q0 + G4_QROWS, :] for r in cls], axis=0)
            kb = jnp.concatenate([kk_ref[r, k0:k0 + 2 * G4_QROWS, :] for r in cls], axis=0)
            vb = jnp.concatenate([vv_ref[r, k0:k0 + 2 * G4_QROWS, :] for r in cls], axis=0)
            o, lse = _attend(qb, kb, vb, bias4[min(qi, 1)])
            for c, r in enumerate(cls):
                o4_ref[r, q0:q0 + G4_QROWS, :] = o[c * G4_QROWS:(c + 1) * G4_QROWS]
                l4_ref[r, q0:q0 + G4_QROWS, :] = lse[c * G4_QROWS:(c + 1) * G4_QROWS]

    bias1 = (b1_ref[first_span], b1_ref[0])
    for u in range(CLASS_ROWS // G1_QROWS):
        q0 = u * G1_QROWS
        k0 = CLASS_ROWS - G1_QROWS + q0
        qb = jnp.concatenate([q_ref[0, 0, 0, r, q0:q0 + G1_QROWS, :] for r in range(N_CLASS)], axis=0)
        kb = jnp.concatenate([kk_ref[r, k0:k0 + 2 * G1_QROWS, :] for r in range(N_CLASS)], axis=0)
        vb = jnp.concatenate([vv_ref[r, k0:k0 + 2 * G1_QROWS, :] for r in range(N_CLASS)], axis=0)
        o, lse = _attend(qb, kb, vb, bias1[min(u, 1)])
        for r in range(N_CLASS):
            o1_ref[r, q0:q0 + G1_QROWS, :] = o[r * G1_QROWS:(r + 1) * G1_QROWS]
            l1_ref[r, q0:q0 + G1_QROWS, :] = lse[r * G1_QROWS:(r + 1) * G1_QROWS]

    for r in range(N_CLASS):
        la, lb, lc = l16_ref[r], l4_ref[r], l1_ref[r]
        m = jnp.maximum(jnp.maximum(la, lb), lc)
        wa, wb, wc = jnp.exp(la - m), jnp.exp(lb - m), jnp.exp(lc - m)
        o = (wa * o16_ref[r] + wb * o4_ref[r] + wc * o1_ref[r]) / (wa + wb + wc)
        o_ref[0, 0, 0, r] = o.astype(BF16)


def _attention(q, k, v):
    B, n_span = q.shape[0], q.shape[1]
    blk = (1, 1, 1, N_CLASS, CLASS_ROWS, HEAD_DIM)
    cur = pl.BlockSpec(blk, lambda b, n, h: (b, n, h, 0, 0, 0))
    prev = pl.BlockSpec(blk, lambda b, n, h: (b, jnp.maximum(n - 1, 0), h, 0, 0, 0))
    b16, b4, b1 = _band_bias_tables()
    const = lambda a: pl.BlockSpec(a.shape, lambda b, n, h: (0,) * a.ndim)
    acc = pltpu.VMEM((N_CLASS, CLASS_ROWS, HEAD_DIM), F32)
    stat = pltpu.VMEM((N_CLASS, CLASS_ROWS, 1), F32)
    kv = pltpu.VMEM((N_CLASS, 2 * CLASS_ROWS, HEAD_DIM), BF16)
    return pl.pallas_call(
        _attn_kernel,
        grid=(B, n_span, N_HEADS),
        in_specs=[cur, cur, prev, cur, prev, const(b16), const(b4), const(b1)],
        out_specs=cur,
        out_shape=jax.ShapeDtypeStruct(q.shape, BF16),
        scratch_shapes=[kv, kv, acc, acc, acc, stat, stat, stat],
        compiler_params=pltpu.CompilerParams(dimension_semantics=("arbitrary", "arbitrary", "arbitrary"),
                                             vmem_limit_bytes=VMEM_LIMIT),
        name="dilated_attention",
    )(q, k, k, v, v, b16, b4, b1)


def _out_kernel(o_ref, za_ref, ga_ref, yr_ref, x_ref, gate_ref, bga_ref, wa_ref, wo_ref, gfin_ref,
                out_ref, op_ref, slab_ref, *, final_norm):
    for hh in range(N_HEADS):
        for r in range(N_CLASS):
            op_ref[r * TILE_CLASS_ROWS:(r + 1) * TILE_CLASS_ROWS, hh * HEAD_DIM:(hh + 1) * HEAD_DIM] = (
                o_ref[0, 0, hh, r])
    za = jnp.concatenate([za_ref[0, 0, r] for r in range(N_CLASS)], axis=0).astype(F32)
    ga = jnp.concatenate([ga_ref[0, 0, r] for r in range(N_CLASS)], axis=0).astype(F32)
    t = (op_ref[...].astype(F32) * _silu(za)).astype(BF16)
    ya = jnp.dot(t, wa_ref[...], preferred_element_type=F32)
    ya = _sigmoid(ga + bga_ref[...]) * ya

    for hh in range(N_HEADS):
        for r in range(N_CLASS):
            slab_ref[hh, pl.ds(r, TILE_CLASS_ROWS, stride=N_CLASS), :] = (
                ya[r * TILE_CLASS_ROWS:(r + 1) * TILE_CLASS_ROWS, hh * HEAD_DIM:(hh + 1) * HEAD_DIM])
    ya_tok = jnp.concatenate([slab_ref[hh] for hh in range(N_HEADS)], axis=1)

    merged = (ya_tok + yr_ref[0].astype(F32)).astype(BF16)
    y = x_ref[0] + gate_ref[0] * jnp.dot(merged, wo_ref[...], preferred_element_type=F32)
    if final_norm:
        ms = jnp.mean(y * y, axis=-1, keepdims=True)
        y = (y * lax.rsqrt(ms + NORM_EPS)) * gfin_ref[...]
    out_ref[0] = y


def _output_stage(o, za, ga, yr, x, gate, bg_a, w_out_attn, w_o, g_final, final_norm):
    B, S, D = x.shape
    tok_spec = pl.BlockSpec((1, TILE, D), lambda b, i: (b, i, 0))
    head_spec = pl.BlockSpec((1, 1, N_HEADS, N_CLASS, TILE_CLASS_ROWS, HEAD_DIM),
                             lambda b, i: (b, i // TILES_PER_SPAN, 0, 0, i % TILES_PER_SPAN, 0))
    rows_spec = pl.BlockSpec((1, 1, N_CLASS, TILE_CLASS_ROWS, D),
                             lambda b, i: (b, i // TILES_PER_SPAN, 0, i % TILES_PER_SPAN, 0))
    const = lambda shape: pl.BlockSpec(shape, lambda b, i: (0,) * len(shape))
    return pl.pallas_call(
        functools.partial(_out_kernel, final_norm=final_norm),
        grid=(B, S // TILE),
        in_specs=[head_spec, rows_spec, rows_spec, tok_spec, tok_spec,
                  pl.BlockSpec((1, 1, D), lambda b, i: (b, 0, 0)),
                  const((1, D)), const((D, D)), const((D, D)), const((1, D))],
        out_specs=tok_spec,
        out_shape=jax.ShapeDtypeStruct((B, S, D), x.dtype),
        scratch_shapes=[pltpu.VMEM((TILE, D), BF16),
                        pltpu.VMEM((N_HEADS, TILE, HEAD_DIM), F32)],
        compiler_params=pltpu.CompilerParams(dimension_semantics=("arbitrary", "arbitrary"),
                                             vmem_limit_bytes=VMEM_LIMIT),
        name="output_stage",
    )(o, za, ga, yr, x, gate, bg_a, w_out_attn, w_o, g_final)


def kernel(x, c, positions, g_norm, w_mod, b_mod, w_in, b_gate, conv_w, conv_b, w_a, b_a, w_x, b_x, lam,
           w_out_rnn, w_out_attn, w_o, g_final):
    B, S, D = x.shape
    depth = g_norm.shape[0]
    assert D == D_MODEL and S % SPAN == 0
    pos3 = positions.reshape(B, S // LANE, LANE)
    for l in range(depth):
        mod = _modulation(c, w_mod[l], b_mod[l])
        shift, scale, gate = (m.reshape(B, 1, D) for m in jnp.split(mod, 3, axis=-1))

        wl = w_in[l].astype(BF16)
        cols = [wl[:, j * D:(j + 1) * D] for j in range(8)]
        w_nat = jnp.concatenate([cols[0], cols[1], cols[6]], axis=1)
        w_perm = jnp.concatenate([cols[2], cols[3], cols[4], cols[5], cols[7]], axis=1)
        xr, zr, gr, q, k, v, za, ga = _in_projection(
            x, shift, scale, g_norm[l].reshape(1, D), pos3, w_nat, w_perm)

        w_gates = jnp.concatenate([w_a[l], w_x[l]], axis=-1).astype(BF16)
        yr = _rnn_branch(xr, zr, gr, pos3, conv_w[l], conv_b[l].reshape(1, D), w_gates,
                         b_a[l].reshape(1, D), b_x[l].reshape(1, D), lam[l].reshape(1, D),
                         b_gate[l, :D].reshape(1, D), w_out_rnn[l].astype(BF16))

        o = _attention(q, k, v)

        x = _output_stage(o, za, ga, yr, x, gate, b_gate[l, D:].reshape(1, D),
                          w_out_attn[l].astype(BF16), w_o[l].astype(BF16), g_final.reshape(1, D),
                          final_norm=(l == depth - 1))
    return x
```

```python
import functools

import numpy as np
import jax
import jax.numpy as jnp
from jax import lax
from jax.experimental import pallas as pl
from jax.experimental.pallas import tpu as pltpu

F32 = jnp.float32
BF16 = jnp.bfloat16

D_MODEL = 1024
N_HEADS = 8
HEAD_DIM = 128
ROT_DIM = 32
ROPE_THETA = 500000.0
CONV_W = 4
LRU_C = 8.0
RNN_BLOCKS = 8
RNN_BLOCK_W = 128
NORM_EPS = 1e-6
NEG_INF = -1e30

SPAN = 2048
N_CLASS = 16
CLASS_ROWS = SPAN // N_CLASS
WINDOW_KEYS = 128
TILE = 512
TILE_CLASS_ROWS = TILE // N_CLASS
TILES_PER_SPAN = SPAN // TILE
LANE = 128
SUBLANE = 8
VMEM_LIMIT = 56 * 1024 * 1024

COL_X_RNN, COL_Z_RNN, COL_Q, COL_K, COL_V, COL_Z_ATTN, COL_G_RNN, COL_G_ATTN = range(8)


def _sigmoid(x):
    return 0.5 * jnp.tanh(0.5 * x) + 0.5


def _silu(x):
    hx = 0.5 * x
    return hx * jnp.tanh(hx) + hx


def _rows_as_lanes_to_rows(row):
    return jnp.broadcast_to(row, (LANE, LANE)).T


def _class_chunks(t):
    return [t[r * TILE_CLASS_ROWS:(r + 1) * TILE_CLASS_ROWS] for r in range(N_CLASS)]


def _mod_kernel(c_ref, w_ref, b_ref, o_ref):
    ca = _silu(c_ref[...])
    o_ref[...] = jnp.dot(ca.astype(BF16), w_ref[...], preferred_element_type=F32) + b_ref[...]


def _modulation(c, w_mod, b_mod):
    B = c.shape[0]
    return pl.pallas_call(
        _mod_kernel,
        out_shape=jax.ShapeDtypeStruct((B, 3 * D_MODEL), F32),
        compiler_params=pltpu.CompilerParams(vmem_limit_bytes=VMEM_LIMIT),
        name="adaln_mod",
    )(c, w_mod.astype(BF16), b_mod.reshape(1, -1))


def _proj_rnn_kernel(x_ref, shift_ref, scale_ref, gnorm_ref, pos_ref, rope_ref, win_ref,
                     convw_ref, convb_ref, wg_ref, ba_ref, bx_ref, lam_ref, bgr_ref, wout_ref,
                     yr_ref, q_ref, k_ref, v_ref, za_ref, ga_ref,
                     slab_ref, xp_ref, pslab_ref, tail_ref, carry_ref):
    i = pl.program_id(1)

    @pl.when(i == 0)
    def _():
        tail_ref[...] = jnp.zeros(tail_ref.shape, F32)
        carry_ref[...] = jnp.zeros(carry_ref.shape, F32)

    x = x_ref[0]
    for hh in range(N_HEADS):
        slab_ref[hh] = x[:, hh * HEAD_DIM:(hh + 1) * HEAD_DIM]
    for hh in range(N_HEADS):
        for r in range(N_CLASS):
            xp_ref[r * TILE_CLASS_ROWS:(r + 1) * TILE_CLASS_ROWS, hh * HEAD_DIM:(hh + 1) * HEAD_DIM] = (
                slab_ref[hh, pl.ds(r, TILE_CLASS_ROWS, stride=N_CLASS), :])
    xp = xp_ref[...]
    ms = jnp.mean(xp * xp, axis=-1, keepdims=True)
    h = (xp * lax.rsqrt(ms + NORM_EPS)) * gnorm_ref[...]
    h = h * (1.0 + scale_ref[0]) + shift_ref[0]
    hb = h.astype(BF16)

    def proj(c):
        return jnp.dot(hb, win_ref[:, c * D_MODEL:(c + 1) * D_MODEL], preferred_element_type=F32)

    for c in range(TILE // LANE):
        prow = pos_ref[0, pl.ds(i * (TILE // LANE) + c, 1), :].astype(F32)
        pslab_ref[c * LANE:(c + 1) * LANE, :] = _rows_as_lanes_to_rows(prow)
    pp = jnp.concatenate(
        [pslab_ref[pl.ds(r, TILE_CLASS_ROWS, stride=N_CLASS), :] for r in range(N_CLASS)], axis=0)

    xs = _class_chunks(proj(COL_X_RNN))
    row = lax.broadcasted_iota(jnp.int32, (TILE_CLASS_ROWS, D_MODEL), 0)

    def wrapped(chunk, first_row):
        return jnp.where(row == 0, first_row, pltpu.roll(chunk, 1, 0))

    taps = [wrapped(xs[N_CLASS - (CONV_W - 1) + t], tail_ref[t:t + 1, :]) for t in range(CONV_W - 1)] + xs
    for t in range(CONV_W - 1):
        tail_ref[t:t + 1, :] = xs[N_CLASS - (CONV_W - 1) + t][TILE_CLASS_ROWS - 1:TILE_CLASS_ROWS, :]
    xcs = []
    for r in range(N_CLASS):
        acc = convb_ref[...] + convw_ref[0:1, :] * taps[r]
        for k in range(1, CONV_W):
            acc = acc + convw_ref[k:k + 1, :] * taps[r + k]
        xcs.append(acc)
    xc = jnp.concatenate(xcs, axis=0)

    xcb = xc.astype(BF16)
    rs, gs = [], []
    for blk in range(RNN_BLOCKS):
        g = jnp.dot(xcb[:, blk * RNN_BLOCK_W:(blk + 1) * RNN_BLOCK_W], wg_ref[blk],
                    preferred_element_type=F32)
        rs.append(g[:, :RNN_BLOCK_W])
        gs.append(g[:, RNN_BLOCK_W:])
    rgate = _sigmoid(jnp.concatenate(rs, axis=1) + ba_ref[...])
    igate = _sigmoid(jnp.concatenate(gs, axis=1) + bx_ref[...])

    nlam = -lam_ref[...]
    softplus = jnp.maximum(nlam, 0.0) + jnp.log(1.0 + jnp.exp(-jnp.abs(nlam)))
    keep = jnp.where(pp == 0.0, 0.0, 1.0)
    keep = jnp.concatenate([keep] * (D_MODEL // LANE), axis=1)
    a = jnp.exp(((-LRU_C) * softplus) * rgate) * keep
    bv = jnp.sqrt(1.0 - a * a) * igate * xc

    a_c, b_c = _class_chunks(a), _class_chunks(bv)
    loc, cum = [b_c[0]], [a_c[0]]
    for r in range(1, N_CLASS):
        loc.append(a_c[r] * loc[-1] + b_c[r])
        cum.append(a_c[r] * cum[-1])
    seg_a, seg_b = cum[-1], loc[-1]
    d = 1
    while d < TILE_CLASS_ROWS:
        take = row >= d
        seg_b = jnp.where(take, seg_a * pltpu.roll(seg_b, d, 0) + seg_b, seg_b)
        seg_a = jnp.where(take, seg_a * pltpu.roll(seg_a, d, 0), seg_a)
        d *= 2
    carry = carry_ref[...]
    seg_end = seg_a * carry + seg_b
    seg_in = jnp.where(row == 0, carry, pltpu.roll(seg_end, 1, 0))
    carry_ref[...] = seg_end[TILE_CLASS_ROWS - 1:TILE_CLASS_ROWS, :]
    hr = jnp.concatenate([loc[r] + cum[r] * seg_in for r in range(N_CLASS)], axis=0)

    y = (hr * _silu(proj(COL_Z_RNN))).astype(BF16)
    y = jnp.dot(y, wout_ref[...], preferred_element_type=F32)
    y = (_sigmoid(proj(COL_G_RNN) + bgr_ref[...]) * y).astype(BF16)
    for r in range(N_CLASS):
        yr_ref[0, 0, r] = y[r * TILE_CLASS_ROWS:(r + 1) * TILE_CLASS_ROWS]

    ang = pp * rope_ref[0:1, :]
    cos_t = jnp.cos(ang)
    sin_t = jnp.sin(ang)
    sin_lo = sin_t * rope_ref[1:2, :]
    sin_hi = sin_t * rope_ref[2:3, :]

    def rope(t):
        return t * cos_t + pltpu.roll(t, HEAD_DIM - ROT_DIM // 2, 1) * sin_lo + pltpu.roll(t, ROT_DIM // 2, 1) * sin_hi

    def store_heads(ref, res, rotary, scale):
        for hh in range(N_HEADS):
            t = res[:, hh * HEAD_DIM:(hh + 1) * HEAD_DIM]
            if rotary:
                t = rope(t)
            if scale is not None:
                t = t * scale
            t = t.astype(BF16)
            for r in range(N_CLASS):
                ref[0, 0, hh, r] = t[r * TILE_CLASS_ROWS:(r + 1) * TILE_CLASS_ROWS]

    def store_rows(ref, res):
        t = res.astype(BF16)
        for r in range(N_CLASS):
            ref[0, 0, r] = t[r * TILE_CLASS_ROWS:(r + 1) * TILE_CLASS_ROWS]

    store_heads(q_ref, proj(COL_Q), True, HEAD_DIM ** -0.5)
    store_heads(k_ref, proj(COL_K), True, None)
    store_heads(v_ref, proj(COL_V), False, None)
    store_rows(za_ref, proj(COL_Z_ATTN))
    store_rows(ga_ref, proj(COL_G_ATTN))


def _rope_lane_table():
    inv_freq = ROPE_THETA ** (-np.arange(0, ROT_DIM, 2, dtype=np.float64) / ROT_DIM)
    half = ROT_DIM // 2
    t = np.zeros((SUBLANE, HEAD_DIM), np.float32)
    t[0, :half] = inv_freq
    t[0, half:ROT_DIM] = inv_freq
    t[1, :half] = -1.0
    t[2, half:ROT_DIM] = 1.0
    return jnp.asarray(t)


def _projection_and_rnn(x, shift, scale, g_norm, pos3, w_in, conv_w, conv_b, w_gates, b_a, b_x, lam, bg_r, w_out):
    B, S, D = x.shape
    n_span = S // SPAN
    head = jax.ShapeDtypeStruct((B, n_span, N_HEADS, N_CLASS, CLASS_ROWS, HEAD_DIM), BF16)
    rows = jax.ShapeDtypeStruct((B, n_span, N_CLASS, CLASS_ROWS, D), BF16)
    tok_spec = pl.BlockSpec((1, TILE, D), lambda b, i: (b, i, 0))
    vec_spec = pl.BlockSpec((1, 1, D), lambda b, i: (b, 0, 0))
    head_spec = pl.BlockSpec((1, 1, N_HEADS, N_CLASS, TILE_CLASS_ROWS, HEAD_DIM),
                             lambda b, i: (b, i // TILES_PER_SPAN, 0, 0, i % TILES_PER_SPAN, 0))
    rows_spec = pl.BlockSpec((1, 1, N_CLASS, TILE_CLASS_ROWS, D),
                             lambda b, i: (b, i // TILES_PER_SPAN, 0, i % TILES_PER_SPAN, 0))
    const = lambda shape: pl.BlockSpec(shape, lambda b, i: (0,) * len(shape), pipeline_mode=pl.Buffered(1))
    return pl.pallas_call(
        _proj_rnn_kernel,
        grid=(B, S // TILE),
        in_specs=[tok_spec, vec_spec, vec_spec, const((1, D)),
                  pl.BlockSpec((1, S // LANE, LANE), lambda b, i: (b, 0, 0)),
                  const((SUBLANE, HEAD_DIM)), const((D, 8 * D)),
                  const((CONV_W, D)), const((1, D)), const((RNN_BLOCKS, RNN_BLOCK_W, 2 * RNN_BLOCK_W)),
                  const((1, D)), const((1, D)), const((1, D)), const((1, D)), const((D, D))],
        out_specs=[rows_spec, head_spec, head_spec, head_spec, rows_spec, rows_spec],
        out_shape=[rows, head, head, head, rows, rows],
        scratch_shapes=[pltpu.VMEM((N_HEADS, TILE, HEAD_DIM), F32),
                        pltpu.VMEM((TILE, D), F32),
                        pltpu.VMEM((TILE, LANE), F32),
                        pltpu.VMEM((SUBLANE, D), F32),
                        pltpu.VMEM((1, D), F32)],
        compiler_params=pltpu.CompilerParams(dimension_semantics=("arbitrary", "arbitrary"),
                                             vmem_limit_bytes=VMEM_LIMIT),
        name="projection_rglru",
    )(x, shift, scale, g_norm, pos3, _rope_lane_table(), w_in, conv_w, conv_b, w_gates, b_a, b_x, lam, bg_r, w_out)


G4_QROWS = 32
G1_QROWS = 16


def _band_bias_tables():
    def table(n_cls, q_rows):
        k_rows = 2 * q_rows
        cq, jq = np.meshgrid(np.arange(n_cls), np.arange(q_rows), indexing="ij")
        ck, jk = np.meshgrid(np.arange(n_cls), np.arange(k_rows), indexing="ij")
        cq, jq, ck, jk = cq.reshape(-1, 1), jq.reshape(-1, 1), ck.reshape(1, -1), jk.reshape(1, -1)
        dist = n_cls * (jq + q_rows - jk) + (cq - ck)
        valid = (dist >= 0) & (dist <= WINDOW_KEYS)
        first = valid & (jk >= q_rows)
        return np.stack([np.where(valid, 0.0, NEG_INF), np.where(first, 0.0, NEG_INF)]).astype(np.float32)

    b16 = table(1, CLASS_ROWS)
    b4 = table(4, G4_QROWS)
    b1 = table(16, G1_QROWS)
    return jnp.asarray(b16), jnp.asarray(b4), jnp.asarray(b1)


def _attend(qb, kb, vb, bias):
    s = lax.dot_general(qb, kb, (((1,), (1,)), ((), ())), preferred_element_type=F32) + bias
    m = jnp.max(s, axis=-1, keepdims=True)
    p = jnp.exp(s - m)
    l = jnp.sum(p, axis=-1, keepdims=True)
    o = jnp.dot(p.astype(BF16), vb, preferred_element_type=F32) / l
    return o, m + jnp.log(l)


def _attn_kernel(q_ref, kc_ref, kp_ref, vc_ref, vp_ref, b16_ref, b4_ref, b1_ref, o_ref,
                 kk_ref, vv_ref, o16_ref, o4_ref, o1_ref, l16_ref, l4_ref, l1_ref):
    first_span = (pl.program_id(1) == 0).astype(jnp.int32)

    for r in range(N_CLASS):
        kk_ref[r, 0:CLASS_ROWS] = kp_ref[0, 0, 0, r]
        kk_ref[r, CLASS_ROWS:2 * CLASS_ROWS] = kc_ref[0, 0, 0, r]
        vv_ref[r, 0:CLASS_ROWS] = vp_ref[0, 0, 0, r]
        vv_ref[r, CLASS_ROWS:2 * CLASS_ROWS] = vc_ref[0, 0, 0, r]

    bias16 = b16_ref[first_span]
    for r in range(N_CLASS):
        o, lse = _attend(q_ref[0, 0, 0, r], kk_ref[r], vv_ref[r], bias16)
        o16_ref[r] = o
        l16_ref[r] = lse

    bias4 = (b4_ref[first_span], b4_ref[0])
    for r4 in range(4):
        cls = [r4 + 4 * c for c in range(4)]
        for qi in range(CLASS_ROWS // G4_QROWS):
            q0 = qi * G4_QROWS
            k0 = CLASS_ROWS - G4_QROWS + q0
            qb = jnp.concatenate([q_ref[0, 0, 0, r, q0:q0 + G4_QROWS, :] for r in cls], axis=0)
            kb = jnp.concatenate([kk_ref[r, k0:k0 + 2 * G4_QROWS, :] for r in cls], axis=0)
            vb = jnp.concatenate([vv_ref[r, k0:k0 + 2 * G4_QROWS, :] for r in cls], axis=0)
            o, lse = _attend(qb, kb, vb, bias4[min(qi, 1)])
            for c, r in enumerate(cls):
                o4_ref[r, q0:q0 + G4_QROWS, :] = o[c * G4_QROWS:(c + 1) * G4_QROWS]
                l4_ref[r, q0:q0 + G4_QROWS, :] = lse[c * G4_QROWS:(c + 1) * G4_QROWS]

    bias1 = (b1_ref[first_span], b1_ref[0])
    for u in range(CLASS_ROWS // G1_QROWS):
        q0 = u * G1_QROWS
        k0 = CLASS_ROWS - G1_QROWS + q0
        qb = jnp.concatenate([q_ref[0, 0, 0, r, q0:q0 + G1_QROWS, :] for r in range(N_CLASS)], axis=0)
        kb = jnp.concatenate([kk_ref[r, k0:k0 + 2 * G1_QROWS, :] for r in range(N_CLASS)], axis=0)
        vb = jnp.concatenate([vv_ref[r, k0:k0 + 2 * G1_QROWS, :] for r in range(N_CLASS)], axis=0)
        o, lse = _attend(qb, kb, vb, bias1[min(u, 1)])
        for r in range(N_CLASS):
            o1_ref[r, q0:q0 + G1_QROWS, :] = o[r * G1_QROWS:(r + 1) * G1_QROWS]
            l1_ref[r, q0:q0 + G1_QROWS, :] = lse[r * G1_QROWS:(r + 1) * G1_QROWS]

    for r in range(N_CLASS):
        la, lb, lc = l16_ref[r], l4_ref[r], l1_ref[r]
        m = jnp.maximum(jnp.maximum(la, lb), lc)
        wa, wb, wc = jnp.exp(la - m), jnp.exp(lb - m), jnp.exp(lc - m)
        o = (wa * o16_ref[r] + wb * o4_ref[r] + wc * o1_ref[r]) / (wa + wb + wc)
        o_ref[0, 0, 0, r] = o.astype(BF16)


def _attention(q, k, v):
    B, n_span = q.shape[0], q.shape[1]
    blk = (1, 1, 1, N_CLASS, CLASS_ROWS, HEAD_DIM)
    cur = pl.BlockSpec(blk, lambda b, n, h: (b, n, h, 0, 0, 0))
    prev = pl.BlockSpec(blk, lambda b, n, h: (b, jnp.maximum(n - 1, 0), h, 0, 0, 0))
    b16, b4, b1 = _band_bias_tables()
    const = lambda a: pl.BlockSpec(a.shape, lambda b, n, h: (0,) * a.ndim)
    acc = pltpu.VMEM((N_CLASS, CLASS_ROWS, HEAD_DIM), F32)
    stat = pltpu.VMEM((N_CLASS, CLASS_ROWS, 1), F32)
    kv = pltpu.VMEM((N_CLASS, 2 * CLASS_ROWS, HEAD_DIM), BF16)
    return pl.pallas_call(
        _attn_kernel,
        grid=(B, n_span, N_HEADS),
        in_specs=[cur, cur, prev, cur, prev, const(b16), const(b4), const(b1)],
        out_specs=cur,
        out_shape=jax.ShapeDtypeStruct(q.shape, BF16),
        scratch_shapes=[kv, kv, acc, acc, acc, stat, stat, stat],
        compiler_params=pltpu.CompilerParams(dimension_semantics=("arbitrary", "arbitrary", "arbitrary"),
                                             vmem_limit_bytes=VMEM_LIMIT),
        name="dilated_attention",
    )(q, k, k, v, v, b16, b4, b1)


def _out_kernel(o_ref, za_ref, ga_ref, yr_ref, x_ref, gate_ref, bga_ref, wa_ref, wo_ref, gfin_ref,
                out_ref, op_ref, slab_ref, *, final_norm):
    for hh in range(N_HEADS):
        for r in range(N_CLASS):
            op_ref[r * TILE_CLASS_ROWS:(r + 1) * TILE_CLASS_ROWS, hh * HEAD_DIM:(hh + 1) * HEAD_DIM] = (
                o_ref[0, 0, hh, r])
    za = jnp.concatenate([za_ref[0, 0, r] for r in range(N_CLASS)], axis=0).astype(F32)
    ga = jnp.concatenate([ga_ref[0, 0, r] for r in range(N_CLASS)], axis=0).astype(F32)
    yr = jnp.concatenate([yr_ref[0, 0, r] for r in range(N_CLASS)], axis=0).astype(F32)
    t = (op_ref[...].astype(F32) * _silu(za)).astype(BF16)
    ya = jnp.dot(t, wa_ref[...], preferred_element_type=F32)
    merged = (_sigmoid(ga + bga_ref[...]) * ya + yr).astype(BF16)
    t = jnp.dot(merged, wo_ref[...], preferred_element_type=F32)

    for hh in range(N_HEADS):
        for r in range(N_CLASS):
            slab_ref[hh, pl.ds(r, TILE_CLASS_ROWS, stride=N_CLASS), :] = (
                t[r * TILE_CLASS_ROWS:(r + 1) * TILE_CLASS_ROWS, hh * HEAD_DIM:(hh + 1) * HEAD_DIM])
    t_tok = jnp.concatenate([slab_ref[hh] for hh in range(N_HEADS)], axis=1)

    y = x_ref[0] + gate_ref[0] * t_tok
    if final_norm:
        ms = jnp.mean(y * y, axis=-1, keepdims=True)
        y = (y * lax.rsqrt(ms + NORM_EPS)) * gfin_ref[...]
    out_ref[0] = y


def _output_stage(o, za, ga, yr, x, gate, bg_a, w_out_attn, w_o, g_final, final_norm):
    B, S, D = x.shape
    tok_spec = pl.BlockSpec((1, TILE, D), lambda b, i: (b, i, 0))
    head_spec = pl.BlockSpec((1, 1, N_HEADS, N_CLASS, TILE_CLASS_ROWS, HEAD_DIM),
                             lambda b, i: (b, i // TILES_PER_SPAN, 0, 0, i % TILES_PER_SPAN, 0))
    rows_spec = pl.BlockSpec((1, 1, N_CLASS, TILE_CLASS_ROWS, D),
                             lambda b, i: (b, i // TILES_PER_SPAN, 0, i % TILES_PER_SPAN, 0))
    const = lambda shape: pl.BlockSpec(shape, lambda b, i: (0,) * len(shape))
    return pl.pallas_call(
        functools.partial(_out_kernel, final_norm=final_norm),
        grid=(B, S // TILE),
        in_specs=[head_spec, rows_spec, rows_spec, rows_spec, tok_spec,
                  pl.BlockSpec((1, 1, D), lambda b, i: (b, 0, 0)),
                  const((1, D)), const((D, D)), const((D, D)), const((1, D))],
        out_specs=tok_spec,
        out_shape=jax.ShapeDtypeStruct((B, S, D), x.dtype),
        scratch_shapes=[pltpu.VMEM((TILE, D), BF16),
                        pltpu.VMEM((N_HEADS, TILE, HEAD_DIM), F32)],
        compiler_params=pltpu.CompilerParams(dimension_semantics=("arbitrary", "arbitrary"),
                                             vmem_limit_bytes=VMEM_LIMIT),
        name="output_stage",
    )(o, za, ga, yr, x, gate, bg_a, w_out_attn, w_o, g_final)


def kernel(x, c, positions, g_norm, w_mod, b_mod, w_in, b_gate, conv_w, conv_b, w_a, b_a, w_x, b_x, lam,
           w_out_rnn, w_out_attn, w_o, g_final):
    B, S, D = x.shape
    depth = g_norm.shape[0]
    assert D == D_MODEL and S % SPAN == 0
    pos3 = positions.reshape(B, S // LANE, LANE)
    for l in range(depth):
        mod = _modulation(c, w_mod[l], b_mod[l])
        shift, scale, gate = (m.reshape(B, 1, D) for m in jnp.split(mod, 3, axis=-1))

        w_gates = jnp.concatenate([w_a[l], w_x[l]], axis=-1).astype(BF16)
        yr, q, k, v, za, ga = _projection_and_rnn(
            x, shift, scale, g_norm[l].reshape(1, D), pos3, w_in[l].astype(BF16),
            conv_w[l], conv_b[l].reshape(1, D), w_gates,
            b_a[l].reshape(1, D), b_x[l].reshape(1, D), lam[l].reshape(1, D),
            b_gate[l, :D].reshape(1, D), w_out_rnn[l].astype(BF16))

        o = _attention(q, k, v)

        x = _output_stage(o, za, ga, yr, x, gate, b_gate[l, D:].reshape(1, D),
                          w_out_attn[l].astype(BF16), w_o[l].astype(BF16), g_final.reshape(1, D),
                          final_norm=(l == depth - 1))
    return x
```

```python
import functools

import numpy as np
import jax
import jax.numpy as jnp
from jax import lax
from jax.experimental import pallas as pl
from jax.experimental.pallas import tpu as pltpu

F32 = jnp.float32
BF16 = jnp.bfloat16

D_MODEL = 1024
N_HEADS = 8
HEAD_DIM = 128
ROT_DIM = 32
ROPE_THETA = 500000.0
CONV_W = 4
LRU_C = 8.0
RNN_BLOCKS = 8
RNN_BLOCK_W = 128
NORM_EPS = 1e-6
NEG_INF = -1e30
LOG2_E = 1.4426950408889634
LN_2 = 0.6931471805599453

SPAN = 2048
N_CLASS = 16
CLASS_ROWS = SPAN // N_CLASS
WINDOW_KEYS = 128
TILE = 512
TILE_CLASS_ROWS = TILE // N_CLASS
TILES_PER_SPAN = SPAN // TILE
LANE = 128
SUBLANE = 8
VMEM_LIMIT = 56 * 1024 * 1024

COL_X_RNN, COL_Z_RNN, COL_Q, COL_K, COL_V, COL_Z_ATTN, COL_G_RNN, COL_G_ATTN = range(8)


def _sigmoid(x):
    return 0.5 * jnp.tanh(0.5 * x) + 0.5


def _silu(x):
    hx = 0.5 * x
    return hx * jnp.tanh(hx) + hx


def _rows_as_lanes_to_rows(row):
    return jnp.broadcast_to(row, (LANE, LANE)).T


def _class_chunks(t):
    return [t[r * TILE_CLASS_ROWS:(r + 1) * TILE_CLASS_ROWS] for r in range(N_CLASS)]


def _mod_kernel(c_ref, w_ref, b_ref, o_ref):
    ca = _silu(c_ref[...])
    o_ref[...] = jnp.dot(ca.astype(BF16), w_ref[...], preferred_element_type=F32) + b_ref[...]


def _modulation(c, w_mod, b_mod):
    B = c.shape[0]
    return pl.pallas_call(
        _mod_kernel,
        out_shape=jax.ShapeDtypeStruct((B, 3 * D_MODEL), F32),
        compiler_params=pltpu.CompilerParams(vmem_limit_bytes=VMEM_LIMIT),
        name="adaln_mod",
    )(c, w_mod.astype(BF16), b_mod.reshape(1, -1))


def _proj_rnn_kernel(x_ref, shift_ref, scale_ref, gnorm_ref, pos_ref, rope_ref, win_ref,
                     convw_ref, convb_ref, wg_ref, ba_ref, bx_ref, lam_ref, bgr_ref, wout_ref,
                     yr_ref, q_ref, k_ref, v_ref, q2_ref, k2_ref, v2_ref, za_ref, ga_ref,
                     slab_ref, xp_ref, pslab_ref, ilv_ref, tail_ref, carry_ref):
    i = pl.program_id(1)

    @pl.when(i == 0)
    def _():
        tail_ref[...] = jnp.zeros(tail_ref.shape, F32)
        carry_ref[...] = jnp.zeros(carry_ref.shape, F32)

    x = x_ref[0]
    for hh in range(N_HEADS):
        slab_ref[hh] = x[:, hh * HEAD_DIM:(hh + 1) * HEAD_DIM]
    for hh in range(N_HEADS):
        for r in range(N_CLASS):
            xp_ref[r * TILE_CLASS_ROWS:(r + 1) * TILE_CLASS_ROWS, hh * HEAD_DIM:(hh + 1) * HEAD_DIM] = (
                slab_ref[hh, pl.ds(r, TILE_CLASS_ROWS, stride=N_CLASS), :])
    xp = xp_ref[...]
    ms = jnp.mean(xp * xp, axis=-1, keepdims=True)
    h = (xp * lax.rsqrt(ms + NORM_EPS)) * gnorm_ref[...]
    h = h * (1.0 + scale_ref[0]) + shift_ref[0]
    hb = h.astype(BF16)

    def proj(c):
        return jnp.dot(hb, win_ref[:, c * D_MODEL:(c + 1) * D_MODEL], preferred_element_type=F32)

    for c in range(TILE // LANE):
        prow = pos_ref[0, pl.ds(i * (TILE // LANE) + c, 1), :].astype(F32)
        pslab_ref[c * LANE:(c + 1) * LANE, :] = _rows_as_lanes_to_rows(prow)
    pp = jnp.concatenate(
        [pslab_ref[pl.ds(r, TILE_CLASS_ROWS, stride=N_CLASS), :] for r in range(N_CLASS)], axis=0)

    xs = _class_chunks(proj(COL_X_RNN))
    row = lax.broadcasted_iota(jnp.int32, (TILE_CLASS_ROWS, D_MODEL), 0)

    def wrapped(chunk, first_row):
        return jnp.where(row == 0, first_row, pltpu.roll(chunk, 1, 0))

    taps = [wrapped(xs[N_CLASS - (CONV_W - 1) + t], tail_ref[t:t + 1, :]) for t in range(CONV_W - 1)] + xs
    for t in range(CONV_W - 1):
        tail_ref[t:t + 1, :] = xs[N_CLASS - (CONV_W - 1) + t][TILE_CLASS_ROWS - 1:TILE_CLASS_ROWS, :]
    xcs = []
    for r in range(N_CLASS):
        acc = convb_ref[...] + convw_ref[0:1, :] * taps[r]
        for k in range(1, CONV_W):
            acc = acc + convw_ref[k:k + 1, :] * taps[r + k]
        xcs.append(acc)
    xc = jnp.concatenate(xcs, axis=0)

    xcb = xc.astype(BF16)
    rs, gs = [], []
    for blk in range(RNN_BLOCKS):
        g = jnp.dot(xcb[:, blk * RNN_BLOCK_W:(blk + 1) * RNN_BLOCK_W], wg_ref[blk],
                    preferred_element_type=F32)
        rs.append(g[:, :RNN_BLOCK_W])
        gs.append(g[:, RNN_BLOCK_W:])
    rgate = _sigmoid(jnp.concatenate(rs, axis=1) + ba_ref[...])
    igate = _sigmoid(jnp.concatenate(gs, axis=1) + bx_ref[...])

    nlam = -lam_ref[...]
    softplus = jnp.maximum(nlam, 0.0) + jnp.log(1.0 + jnp.exp(-jnp.abs(nlam)))
    keep = jnp.where(pp == 0.0, 0.0, 1.0)
    keep = jnp.concatenate([keep] * (D_MODEL // LANE), axis=1)
    a = jnp.exp(((-LRU_C) * softplus) * rgate) * keep
    bv = jnp.sqrt(1.0 - a * a) * igate * xc

    a_c, b_c = _class_chunks(a), _class_chunks(bv)
    loc, cum = [b_c[0]], [a_c[0]]
    for r in range(1, N_CLASS):
        loc.append(a_c[r] * loc[-1] + b_c[r])
        cum.append(a_c[r] * cum[-1])
    seg_a, seg_b = cum[-1], loc[-1]
    d = 1
    while d < TILE_CLASS_ROWS:
        take = row >= d
        seg_b = jnp.where(take, seg_a * pltpu.roll(seg_b, d, 0) + seg_b, seg_b)
        seg_a = jnp.where(take, seg_a * pltpu.roll(seg_a, d, 0), seg_a)
        d *= 2
    carry = carry_ref[...]
    seg_end = seg_a * carry + seg_b
    seg_in = jnp.where(row == 0, carry, pltpu.roll(seg_end, 1, 0))
    carry_ref[...] = seg_end[TILE_CLASS_ROWS - 1:TILE_CLASS_ROWS, :]
    hr = jnp.concatenate([loc[r] + cum[r] * seg_in for r in range(N_CLASS)], axis=0)

    y = (hr * _silu(proj(COL_Z_RNN))).astype(BF16)
    y = jnp.dot(y, wout_ref[...], preferred_element_type=F32)
    y = (_sigmoid(proj(COL_G_RNN) + bgr_ref[...]) * y).astype(BF16)
    for r in range(N_CLASS):
        yr_ref[0, 0, r] = y[r * TILE_CLASS_ROWS:(r + 1) * TILE_CLASS_ROWS]

    ang = pp * rope_ref[0:1, :]
    cos_t = jnp.cos(ang)
    sin_t = jnp.sin(ang)
    sin_lo = sin_t * rope_ref[1:2, :]
    sin_hi = sin_t * rope_ref[2:3, :]

    def rope(t):
        return t * cos_t + pltpu.roll(t, HEAD_DIM - ROT_DIM // 2, 1) * sin_lo + pltpu.roll(t, ROT_DIM // 2, 1) * sin_hi

    def store_heads(ref, pair_ref, res, rotary, scale):
        for hh in range(N_HEADS):
            t = res[:, hh * HEAD_DIM:(hh + 1) * HEAD_DIM]
            if rotary:
                t = rope(t)
            if scale is not None:
                t = t * scale
            tb = t.astype(BF16)
            for r in range(N_CLASS):
                ref[0, 0, hh, r] = tb[r * TILE_CLASS_ROWS:(r + 1) * TILE_CLASS_ROWS]
            for rr in range(N_CLASS // 2):
                for p in range(2):
                    r = 2 * rr + p
                    ilv_ref[rr, pl.ds(p, TILE_CLASS_ROWS, stride=2), :] = (
                        t[r * TILE_CLASS_ROWS:(r + 1) * TILE_CLASS_ROWS])
                pair_ref[0, 0, hh, rr] = ilv_ref[rr].astype(BF16)

    def store_rows(ref, res):
        t = res.astype(BF16)
        for r in range(N_CLASS):
            ref[0, 0, r] = t[r * TILE_CLASS_ROWS:(r + 1) * TILE_CLASS_ROWS]

    store_heads(q_ref, q2_ref, proj(COL_Q), True, HEAD_DIM ** -0.5 * LOG2_E)
    store_heads(k_ref, k2_ref, proj(COL_K), True, None)
    store_heads(v_ref, v2_ref, proj(COL_V), False, None)
    store_rows(za_ref, proj(COL_Z_ATTN))
    store_rows(ga_ref, proj(COL_G_ATTN))


def _rope_lane_table():
    inv_freq = ROPE_THETA ** (-np.arange(0, ROT_DIM, 2, dtype=np.float64) / ROT_DIM)
    half = ROT_DIM // 2
    t = np.zeros((SUBLANE, HEAD_DIM), np.float32)
    t[0, :half] = inv_freq
    t[0, half:ROT_DIM] = inv_freq
    t[1, :half] = -1.0
    t[2, half:ROT_DIM] = 1.0
    return jnp.asarray(t)


def _projection_and_rnn(x, shift, scale, g_norm, pos3, w_in, conv_w, conv_b, w_gates, b_a, b_x, lam, bg_r, w_out):
    B, S, D = x.shape
    n_span = S // SPAN
    head = jax.ShapeDtypeStruct((B, n_span, N_HEADS, N_CLASS, CLASS_ROWS, HEAD_DIM), BF16)
    pair = jax.ShapeDtypeStruct((B, n_span, N_HEADS, N_CLASS // 2, 2 * CLASS_ROWS, HEAD_DIM), BF16)
    rows = jax.ShapeDtypeStruct((B, n_span, N_CLASS, CLASS_ROWS, D), BF16)
    tok_spec = pl.BlockSpec((1, TILE, D), lambda b, i: (b, i, 0))
    vec_spec = pl.BlockSpec((1, 1, D), lambda b, i: (b, 0, 0))
    head_spec = pl.BlockSpec((1, 1, N_HEADS, N_CLASS, TILE_CLASS_ROWS, HEAD_DIM),
                             lambda b, i: (b, i // TILES_PER_SPAN, 0, 0, i % TILES_PER_SPAN, 0))
    pair_spec = pl.BlockSpec((1, 1, N_HEADS, N_CLASS // 2, 2 * TILE_CLASS_ROWS, HEAD_DIM),
                             lambda b, i: (b, i // TILES_PER_SPAN, 0, 0, i % TILES_PER_SPAN, 0))
    rows_spec = pl.BlockSpec((1, 1, N_CLASS, TILE_CLASS_ROWS, D),
                             lambda b, i: (b, i // TILES_PER_SPAN, 0, i % TILES_PER_SPAN, 0))
    const = lambda shape: pl.BlockSpec(shape, lambda b, i: (0,) * len(shape), pipeline_mode=pl.Buffered(1))
    return pl.pallas_call(
        _proj_rnn_kernel,
        grid=(B, S // TILE),
        in_specs=[tok_spec, vec_spec, vec_spec, const((1, D)),
                  pl.BlockSpec((1, S // LANE, LANE), lambda b, i: (b, 0, 0)),
                  const((SUBLANE, HEAD_DIM)), const((D, 8 * D)),
                  const((CONV_W, D)), const((1, D)), const((RNN_BLOCKS, RNN_BLOCK_W, 2 * RNN_BLOCK_W)),
                  const((1, D)), const((1, D)), const((1, D)), const((1, D)), const((D, D))],
        out_specs=[rows_spec, head_spec, head_spec, head_spec, pair_spec, pair_spec, pair_spec,
                   rows_spec, rows_spec],
        out_shape=[rows, head, head, head, pair, pair, pair, rows, rows],
        scratch_shapes=[pltpu.VMEM((N_HEADS, TILE, HEAD_DIM), F32),
                        pltpu.VMEM((TILE, D), F32),
                        pltpu.VMEM((TILE, LANE), F32),
                        pltpu.VMEM((N_CLASS // 2, 2 * TILE_CLASS_ROWS, HEAD_DIM), F32),
                        pltpu.VMEM((SUBLANE, D), F32),
                        pltpu.VMEM((1, D), F32)],
        compiler_params=pltpu.CompilerParams(dimension_semantics=("arbitrary", "arbitrary"),
                                             vmem_limit_bytes=VMEM_LIMIT),
        name="projection_rglru",
    )(x, shift, scale, g_norm, pos3, _rope_lane_table(), w_in, conv_w, conv_b, w_gates, b_a, b_x, lam, bg_r, w_out)


QBLK = 128
KBLK = 2 * QBLK
G4_QROWS = QBLK // 4
G1_QROWS = QBLK // N_CLASS


def _mask_tables():
    def table(qpos, kpos, kprev, dil):
        dist = qpos.reshape(-1, 1) - kpos.reshape(1, -1)
        valid = (dist >= 0) & (dist <= WINDOW_KEYS * dil)
        first = valid & ~kprev.reshape(1, -1)
        return np.stack([np.where(valid, 0.0, NEG_INF), np.where(first, 0.0, NEG_INF)])

    jq, jk = np.arange(QBLK), np.arange(KBLK) - QBLK
    t16 = table(N_CLASS * jq, N_CLASS * jk, jk < 0, 16)
    cq, aq = np.meshgrid(np.arange(4), np.arange(G4_QROWS), indexing="ij")
    ck, ak = np.meshgrid(np.arange(4), np.arange(2 * G4_QROWS), indexing="ij")
    t4 = table(N_CLASS * aq + 4 * cq, N_CLASS * (ak - G4_QROWS) + 4 * ck, ak < G4_QROWS, 4)
    rq, aq, pq = np.meshgrid(np.arange(N_CLASS // 2), np.arange(G1_QROWS), np.arange(2), indexing="ij")
    rk, ak, pk = np.meshgrid(np.arange(N_CLASS // 2), np.arange(2 * G1_QROWS), np.arange(2), indexing="ij")
    t1 = table(N_CLASS * aq + 2 * rq + pq, N_CLASS * (ak - G1_QROWS) + 2 * rk + pk, ak < G1_QROWS, 1)
    return [jnp.asarray(t, F32) for t in (t16, t4, t1)]


def _attend(qb, kb, vb, mask, ones):
    s = lax.dot_general(qb, kb, (((1,), (1,)), ((), ())), preferred_element_type=F32) + mask
    m = jnp.max(s, axis=-1, keepdims=True)
    p = jnp.exp2(s - m).astype(BF16)
    oe = jnp.dot(p, jnp.concatenate([vb, ones], axis=1), preferred_element_type=F32)
    l = oe[:, HEAD_DIM:]
    return oe[:, :HEAD_DIM] / l, m + jnp.log2(l)


def _attn_kernel(q_ref, k_ref, v_ref, q2_ref, k2_ref, v2_ref, t16_ref, t4_ref, t1_ref, ones_ref,
                 o_ref, kk_ref, vv_ref, kk2_ref, vv2_ref, o16_ref, o4_ref, o1_ref, l16_ref, l4_ref, l1_ref):
    n = pl.program_id(2)
    first_span = (n == 0).astype(jnp.int32)

    @pl.when(n == 0)
    def _():
        kk_ref[:, 0:CLASS_ROWS, :] = jnp.zeros((N_CLASS, CLASS_ROWS, HEAD_DIM), BF16)
        vv_ref[:, 0:CLASS_ROWS, :] = jnp.zeros((N_CLASS, CLASS_ROWS, HEAD_DIM), BF16)
        kk2_ref[:, 0:2 * CLASS_ROWS, :] = jnp.zeros((N_CLASS // 2, 2 * CLASS_ROWS, HEAD_DIM), BF16)
        vv2_ref[:, 0:2 * CLASS_ROWS, :] = jnp.zeros((N_CLASS // 2, 2 * CLASS_ROWS, HEAD_DIM), BF16)

    kk_ref[:, CLASS_ROWS:2 * CLASS_ROWS, :] = k_ref[0, 0, 0]
    vv_ref[:, CLASS_ROWS:2 * CLASS_ROWS, :] = v_ref[0, 0, 0]
    kk2_ref[:, 2 * CLASS_ROWS:4 * CLASS_ROWS, :] = k2_ref[0, 0, 0]
    vv2_ref[:, 2 * CLASS_ROWS:4 * CLASS_ROWS, :] = v2_ref[0, 0, 0]

    ones = ones_ref[...]

    mask16 = t16_ref[first_span]
    for r in range(N_CLASS):
        o16_ref[r], l16_ref[r] = _attend(q_ref[0, 0, 0, r], kk_ref[r], vv_ref[r], mask16, ones)

    mask4 = (t4_ref[first_span], t4_ref[0])
    for r4 in range(4):
        cls = [r4 + 4 * c for c in range(4)]
        for qi in range(CLASS_ROWS // G4_QROWS):
            q0 = qi * G4_QROWS
            k0 = CLASS_ROWS - G4_QROWS + q0
            qb = jnp.concatenate([q_ref[0, 0, 0, r, q0:q0 + G4_QROWS, :] for r in cls], axis=0)
            kb = jnp.concatenate([kk_ref[r, k0:k0 + 2 * G4_QROWS, :] for r in cls], axis=0)
            vb = jnp.concatenate([vv_ref[r, k0:k0 + 2 * G4_QROWS, :] for r in cls], axis=0)
            o, lse = _attend(qb, kb, vb, mask4[min(qi, 1)], ones)
            for c, r in enumerate(cls):
                o4_ref[r, q0:q0 + G4_QROWS, :] = o[c * G4_QROWS:(c + 1) * G4_QROWS]
                l4_ref[r, q0:q0 + G4_QROWS, :] = lse[c * G4_QROWS:(c + 1) * G4_QROWS]

    mask1 = (t1_ref[first_span], t1_ref[0])
    pairs = range(N_CLASS // 2)
    for u in range(CLASS_ROWS // G1_QROWS):
        q0 = u * 2 * G1_QROWS
        k0 = 2 * CLASS_ROWS - 2 * G1_QROWS + q0
        qb = jnp.concatenate([q2_ref[0, 0, 0, rr, q0:q0 + 2 * G1_QROWS, :] for rr in pairs], axis=0)
        kb = jnp.concatenate([kk2_ref[rr, k0:k0 + 4 * G1_QROWS, :] for rr in pairs], axis=0)
        vb = jnp.concatenate([vv2_ref[rr, k0:k0 + 4 * G1_QROWS, :] for rr in pairs], axis=0)
        o, lse = _attend(qb, kb, vb, mask1[min(u, 1)], ones)
        for rr in pairs:
            o1_ref[rr, q0:q0 + 2 * G1_QROWS, :] = o[rr * 2 * G1_QROWS:(rr + 1) * 2 * G1_QROWS]
            l1_ref[rr, q0:q0 + 2 * G1_QROWS, :] = lse[rr * 2 * G1_QROWS:(rr + 1) * 2 * G1_QROWS]

    for r in range(N_CLASS):
        rr, p = divmod(r, 2)
        la, lb, lc = l16_ref[r], l4_ref[r], l1_ref[rr, pl.ds(p, CLASS_ROWS, stride=2), :]
        oc = o1_ref[rr, pl.ds(p, CLASS_ROWS, stride=2), :]
        m = jnp.maximum(jnp.maximum(la, lb), lc)
        wa, wb, wc = jnp.exp2(la - m), jnp.exp2(lb - m), jnp.exp2(lc - m)
        o = (wa * o16_ref[r] + wb * o4_ref[r] + wc * oc) / (wa + wb + wc)
        o_ref[0, 0, 0, r] = o.astype(BF16)

    kk_ref[:, 0:CLASS_ROWS, :] = k_ref[0, 0, 0]
    vv_ref[:, 0:CLASS_ROWS, :] = v_ref[0, 0, 0]
    kk2_ref[:, 0:2 * CLASS_ROWS, :] = k2_ref[0, 0, 0]
    vv2_ref[:, 0:2 * CLASS_ROWS, :] = v2_ref[0, 0, 0]


def _attention(q, k, v, q2, k2, v2):
    B, n_span = q.shape[0], q.shape[1]
    idx = lambda b, h, n: (b, n, h, 0, 0, 0)
    cls_spec = pl.BlockSpec((1, 1, 1, N_CLASS, CLASS_ROWS, HEAD_DIM), idx)
    pair_spec = pl.BlockSpec((1, 1, 1, N_CLASS // 2, 2 * CLASS_ROWS, HEAD_DIM), idx)
    t16, t4, t1 = _mask_tables()
    ones = jnp.ones((KBLK, HEAD_DIM), BF16)
    const = lambda a: pl.BlockSpec(a.shape, lambda b, h, n: (0,) * a.ndim)
    kv = pltpu.VMEM((N_CLASS, 2 * CLASS_ROWS, HEAD_DIM), BF16)
    kv2 = pltpu.VMEM((N_CLASS // 2, 4 * CLASS_ROWS, HEAD_DIM), BF16)
    acc = pltpu.VMEM((N_CLASS, CLASS_ROWS, HEAD_DIM), F32)
    acc2 = pltpu.VMEM((N_CLASS // 2, 2 * CLASS_ROWS, HEAD_DIM), F32)
    return pl.pallas_call(
        _attn_kernel,
        grid=(B, N_HEADS, n_span),
        in_specs=[cls_spec, cls_spec, cls_spec, pair_spec, pair_spec, pair_spec,
                  const(t16), const(t4), const(t1), const(ones)],
        out_specs=cls_spec,
        out_shape=jax.ShapeDtypeStruct(q.shape, BF16),
        scratch_shapes=[kv, kv, kv2, kv2, acc, acc, acc2, acc, acc, acc2],
        compiler_params=pltpu.CompilerParams(dimension_semantics=("arbitrary", "arbitrary", "arbitrary"),
                                             vmem_limit_bytes=VMEM_LIMIT),
        name="dilated_attention",
    )(q, k, v, q2, k2, v2, t16, t4, t1, ones)


def _out_kernel(o_ref, za_ref, ga_ref, yr_ref, x_ref, gate_ref, bga_ref, wa_ref, wo_ref, gfin_ref,
                out_ref, op_ref, slab_ref, *, final_norm):
    for hh in range(N_HEADS):
        for r in range(N_CLASS):
            op_ref[r * TILE_CLASS_ROWS:(r + 1) * TILE_CLASS_ROWS, hh * HEAD_DIM:(hh + 1) * HEAD_DIM] = (
                o_ref[0, 0, hh, r])
    za = jnp.concatenate([za_ref[0, 0, r] for r in range(N_CLASS)], axis=0).astype(F32)
    ga = jnp.concatenate([ga_ref[0, 0, r] for r in range(N_CLASS)], axis=0).astype(F32)
    yr = jnp.concatenate([yr_ref[0, 0, r] for r in range(N_CLASS)], axis=0).astype(F32)
    t = (op_ref[...].astype(F32) * _silu(za)).astype(BF16)
    ya = jnp.dot(t, wa_ref[...], preferred_element_type=F32)
    merged = (_sigmoid(ga + bga_ref[...]) * ya + yr).astype(BF16)
    t = jnp.dot(merged, wo_ref[...], preferred_element_type=F32)

    for hh in range(N_HEADS):
        for r in range(N_CLASS):
            slab_ref[hh, pl.ds(r, TILE_CLASS_ROWS, stride=N_CLASS), :] = (
                t[r * TILE_CLASS_ROWS:(r + 1) * TILE_CLASS_ROWS, hh * HEAD_DIM:(hh + 1) * HEAD_DIM])
    t_tok = jnp.concatenate([slab_ref[hh] for hh in range(N_HEADS)], axis=1)

    y = x_ref[0] + gate_ref[0] * t_tok
    if final_norm:
        ms = jnp.mean(y * y, axis=-1, keepdims=True)
        y = (y * lax.rsqrt(ms + NORM_EPS)) * gfin_ref[...]
    out_ref[0] = y


def _output_stage(o, za, ga, yr, x, gate, bg_a, w_out_attn, w_o, g_final, final_norm):
    B, S, D = x.shape
    tok_spec = pl.BlockSpec((1, TILE, D), lambda b, i: (b, i, 0))
    head_spec = pl.BlockSpec((1, 1, N_HEADS, N_CLASS, TILE_CLASS_ROWS, HEAD_DIM),
                             lambda b, i: (b, i // TILES_PER_SPAN, 0, 0, i % TILES_PER_SPAN, 0))
    rows_spec = pl.BlockSpec((1, 1, N_CLASS, TILE_CLASS_ROWS, D),
                             lambda b, i: (b, i // TILES_PER_SPAN, 0, i % TILES_PER_SPAN, 0))
    const = lambda shape: pl.BlockSpec(shape, lambda b, i: (0,) * len(shape))
    return pl.pallas_call(
        functools.partial(_out_kernel, final_norm=final_norm),
        grid=(B, S // TILE),
        in_specs=[head_spec, rows_spec, rows_spec, rows_spec, tok_spec,
                  pl.BlockSpec((1, 1, D), lambda b, i: (b, 0, 0)),
                  const((1, D)), const((D, D)), const((D, D)), const((1, D))],
        out_specs=tok_spec,
        out_shape=jax.ShapeDtypeStruct((B, S, D), x.dtype),
        scratch_shapes=[pltpu.VMEM((TILE, D), BF16),
                        pltpu.VMEM((N_HEADS, TILE, HEAD_DIM), F32)],
        compiler_params=pltpu.CompilerParams(dimension_semantics=("arbitrary", "arbitrary"),
                                             vmem_limit_bytes=VMEM_LIMIT),
        name="output_stage",
    )(o, za, ga, yr, x, gate, bg_a, w_out_attn, w_o, g_final)


def kernel(x, c, positions, g_norm, w_mod, b_mod, w_in, b_gate, conv_w, conv_b, w_a, b_a, w_x, b_x, lam,
           w_out_rnn, w_out_attn, w_o, g_final):
    B, S, D = x.shape
    depth = g_norm.shape[0]
    assert D == D_MODEL and S % SPAN == 0
    pos3 = positions.reshape(B, S // LANE, LANE)
    for l in range(depth):
        mod = _modulation(c, w_mod[l], b_mod[l])
        shift, scale, gate = (m.reshape(B, 1, D) for m in jnp.split(mod, 3, axis=-1))

        w_gates = jnp.concatenate([w_a[l], w_x[l]], axis=-1).astype(BF16)
        yr, q, k, v, q2, k2, v2, za, ga = _projection_and_rnn(
            x, shift, scale, g_norm[l].reshape(1, D), pos3, w_in[l].astype(BF16),
            conv_w[l], conv_b[l].reshape(1, D), w_gates,
            b_a[l].reshape(1, D), b_x[l].reshape(1, D), lam[l].reshape(1, D),
            b_gate[l, :D].reshape(1, D), w_out_rnn[l].astype(BF16))

        o = _attention(q, k, v, q2, k2, v2)

        x = _output_stage(o, za, ga, yr, x, gate, b_gate[l, D:].reshape(1, D),
                          w_out_attn[l].astype(BF16), w_o[l].astype(BF16), g_final.reshape(1, D),
                          final_norm=(l == depth - 1))
    return x
```
